```python
import jax, jax.numpy as jnp
from jax import lax
import numpy as np

D_MODEL = 1024
BATCH = 8
SEQ = 4096
DEPTH = 1

MEM_LEN = 256
EPS = 1e-6
HEAD_DIM = 64
ATTN_WIDTH = D_MODEL // 2
ATTN_Q_HEADS = ATTN_WIDTH // HEAD_DIM
ATTN_KV_HEADS = 2
ATTN_REP = ATTN_Q_HEADS // ATTN_KV_HEADS
KV_WIDTH = ATTN_KV_HEADS * HEAD_DIM
WINDOW = 128
ATTN_BLOCK = 128
SGU_WIDTH = D_MODEL - ATTN_WIDTH
SGU_GROUP_DIM = 64
SGU_GROUPS = SGU_WIDTH // SGU_GROUP_DIM
SGU_CHUNK = 128
IN_WIDTH = ATTN_WIDTH + 2 * KV_WIDTH + 2 * SGU_WIDTH
X_HEADS = 4
X_HEAD_DIM = D_MODEL // X_HEADS
N_GROUPS = 4
EXPERTS_PER_GROUP = 8
N_EXPERTS = N_GROUPS * EXPERTS_PER_GROUP
TOP_K = 2
D_EXPERT = D_MODEL // 2
MOE_BLOCK = 256

kernel_name = "hymba_swa_sgu_memxattn_hmoe"


def rms_norm(x, gain):
    xf = x.astype(jnp.float32)
    y = xf * lax.rsqrt(jnp.mean(xf * xf, axis=-1, keepdims=True) + EPS)
    return (y * gain.astype(jnp.float32)).astype(x.dtype)


def sliding_window_attention(q, k, v, sinks):
    B, S = q.shape[:2]
    nb = S // ATTN_BLOCK
    qb = q.reshape(B, nb, ATTN_BLOCK, ATTN_KV_HEADS, ATTN_REP, HEAD_DIM).astype(jnp.float32)
    kb = k.reshape(B, nb, ATTN_BLOCK, ATTN_KV_HEADS, HEAD_DIM)
    vb = v.reshape(B, nb, ATTN_BLOCK, ATTN_KV_HEADS, HEAD_DIM)
    pad = ((0, 0), (1, 0), (0, 0), (0, 0), (0, 0))
    kwin = jnp.concatenate([jnp.pad(kb, pad)[:, :-1], kb], axis=2).astype(jnp.float32)
    vwin = jnp.concatenate([jnp.pad(vb, pad)[:, :-1], vb], axis=2).astype(jnp.float32)
    scores = jnp.einsum('bnqhrd,bnkhd->bnhrqk', qb, kwin) * (HEAD_DIM ** -0.5)
    qi = jnp.arange(ATTN_BLOCK)[:, None]
    kj = jnp.arange(2 * ATTN_BLOCK)[None, :]
    dist = qi + ATTN_BLOCK - kj
    band = (dist >= 0) & (dist < WINDOW)
    key_pos = jnp.arange(nb)[:, None, None] * ATTN_BLOCK + kj[None] - ATTN_BLOCK
    mask = band[None] & (key_pos >= 0)
    scores = jnp.where(mask[None, :, None, None], scores, jnp.finfo(jnp.float32).min)
    sink = sinks.astype(jnp.float32).reshape(ATTN_KV_HEADS, ATTN_REP)[None, None, :, :, None, None]
    m = jnp.maximum(jnp.max(scores, axis=-1, keepdims=True), sink)
    p = jnp.exp(scores - m)
    denom = jnp.sum(p, axis=-1, keepdims=True) + jnp.exp(sink - m)
    out = jnp.einsum('bnhrqk,bnkhd->bnqhrd', p / denom, vwin)
    return out.reshape(B, S, ATTN_WIDTH).astype(q.dtype)


def spatial_gating(u, v, sgu_norm, sgu_w, sgu_b):
    B, S = u.shape[:2]
    nc = S // SGU_CHUNK
    vc = rms_norm(v, sgu_norm).reshape(B, nc, SGU_CHUNK, SGU_GROUPS, SGU_GROUP_DIM)
    causal = jnp.tril(jnp.ones((SGU_CHUNK, SGU_CHUNK), dtype=bool))
    w = jnp.where(causal[None], sgu_w, jnp.zeros((), sgu_w.dtype))
    s = jnp.einsum('gij,bcjgd->bcigd', w, vc) + sgu_b.T[None, None, :, :, None]
    return u * s.reshape(B, S, SGU_WIDTH)


def memory_cross_attention(h, mem_n, w_cq, w_ck, w_cv, cq_norm, ck_norm, w_co):
    B, S = h.shape[:2]
    M = mem_n.shape[1]
    q = rms_norm((h @ w_cq).reshape(B, S, X_HEADS, X_HEAD_DIM), cq_norm)
    k = rms_norm((mem_n @ w_ck).reshape(B, M, X_HEADS, X_HEAD_DIM), ck_norm)
    v = (mem_n @ w_cv).reshape(B, M, X_HEADS, X_HEAD_DIM)
    scores = jnp.einsum('bshd,bmhd->bhsm', q.astype(jnp.float32), k.astype(jnp.float32)) * (X_HEAD_DIM ** -0.5)
    p = jax.nn.softmax(scores, axis=-1)
    out = jnp.einsum('bhsm,bmhd->bshd', p, v.astype(jnp.float32)).astype(h.dtype)
    return out.reshape(B, S, D_MODEL) @ w_co


def hierarchical_moe(h, w_router_group, w_router_expert, w1, w3, w2):
    B, S, D = h.shape
    T = B * S
    xt = h.reshape(T, D)
    xf = xt.astype(jnp.float32)
    p_group = jax.nn.softmax(xf @ w_router_group.astype(jnp.float32), axis=-1)
    g_idx = jnp.argmax(p_group, axis=-1)
    g_w = jnp.take_along_axis(p_group, g_idx[:, None], axis=1)[:, 0]
    local = jnp.einsum('td,gde->tge', xf, w_router_expert.astype(jnp.float32))
    local = jnp.take_along_axis(local, g_idx[:, None, None], axis=1)[:, 0]
    top_logit, top_idx = lax.top_k(local, TOP_K)
    gate = jax.nn.softmax(top_logit, axis=-1) * g_w[:, None]
    flat_e = (g_idx[:, None] * EXPERTS_PER_GROUP + top_idx).reshape(-1).astype(jnp.int32)
    flat_w = gate.reshape(-1)
    A = T * TOP_K
    order = jnp.argsort(flat_e)
    sorted_e = flat_e[order]
    counts = jnp.zeros((N_EXPERTS,), jnp.int32).at[flat_e].add(1)
    starts = jnp.cumsum(counts) - counts
    padded = (counts + MOE_BLOCK - 1) // MOE_BLOCK * MOE_BLOCK
    pad_end = jnp.cumsum(padded)
    pad_start = pad_end - padded
    dest = pad_start[sorted_e] + jnp.arange(A, dtype=jnp.int32) - starts[sorted_e]
    nb = -(-A // MOE_BLOCK) + N_EXPERTS
    P = nb * MOE_BLOCK
    slot_tok = jnp.full((P,), T, jnp.int32).at[dest].set((order // TOP_K).astype(jnp.int32))
    slot_w = jnp.zeros((P,), jnp.float32).at[dest].set(flat_w[order])
    block_e = jnp.minimum(jnp.searchsorted(pad_end, jnp.arange(nb, dtype=jnp.int32) * MOE_BLOCK, side='right'), N_EXPERTS - 1)
    x_pad = jnp.concatenate([xt, jnp.zeros((1, D), xt.dtype)], axis=0)
    xb = x_pad[slot_tok].reshape(nb, MOE_BLOCK, D)

    def run_block(args):
        xblk, e = args
        hid = jax.nn.silu(xblk @ w1[e]) * (xblk @ w3[e])
        return hid @ w2[e]

    yb = lax.map(run_block, (xb, block_e)).reshape(P, D)
    y = jnp.zeros((T + 1, D), jnp.float32).at[slot_tok].add(yb.astype(jnp.float32) * slot_w[:, None])[:T]
    return y.astype(h.dtype).reshape(B, S, D)


def setup_inputs(seed: int = 0) -> dict:
    key = jax.random.key(seed)
    ks = jax.random.split(key, 32)
    f32 = jnp.float32
    nrm = lambda k, shape, s: jax.random.normal(k, shape, f32) * s
    gain = lambda k, n: 1.0 + 0.02 * jax.random.normal(k, (n,), f32)
    return {
        "x": nrm(ks[0], (BATCH, SEQ, D_MODEL), 1.0),
        "mem": nrm(ks[1], (BATCH, MEM_LEN, D_MODEL), 1.0),
        "norm_mix": gain(ks[2], D_MODEL),
        "w_in": nrm(ks[3], (D_MODEL, IN_WIDTH), D_MODEL ** -0.5),
        "q_norm": gain(ks[4], HEAD_DIM),
        "k_norm": gain(ks[5], HEAD_DIM),
        "attn_sinks": nrm(ks[6], (ATTN_Q_HEADS,), 0.5),
        "sgu_norm": gain(ks[7], SGU_WIDTH),
        "sgu_w": nrm(ks[8], (SGU_GROUPS, SGU_CHUNK, SGU_CHUNK), SGU_CHUNK ** -0.5),
        "sgu_b": 1.0 + nrm(ks[9], (SGU_GROUPS, SGU_CHUNK), 0.1),
        "out_norm_attn": gain(ks[10], ATTN_WIDTH),
        "out_norm_sgu": gain(ks[11], SGU_WIDTH),
        "w_out": nrm(ks[12], (D_MODEL, D_MODEL), D_MODEL ** -0.5),
        "norm_cross": gain(ks[13], D_MODEL),
        "norm_mem": gain(ks[14], D_MODEL),
        "w_cq": nrm(ks[15], (D_MODEL, D_MODEL), D_MODEL ** -0.5),
        "w_ck": nrm(ks[16], (D_MODEL, D_MODEL), D_MODEL ** -0.5),
        "w_cv": nrm(ks[17], (D_MODEL, D_MODEL), D_MODEL ** -0.5),
        "cq_norm": gain(ks[18], X_HEAD_DIM),
        "ck_norm": gain(ks[19], X_HEAD_DIM),
        "w_co": nrm(ks[20], (D_MODEL, D_MODEL), D_MODEL ** -0.5),
        "norm_ffn": gain(ks[21], D_MODEL),
        "w_router_group": nrm(ks[22], (D_MODEL, N_GROUPS), D_MODEL ** -0.5),
        "w_router_expert": nrm(ks[23], (N_GROUPS, D_MODEL, EXPERTS_PER_GROUP), D_MODEL ** -0.5),
        "w1": nrm(ks[24], (N_EXPERTS, D_MODEL, D_EXPERT), D_MODEL ** -0.5),
        "w3": nrm(ks[25], (N_EXPERTS, D_MODEL, D_EXPERT), D_MODEL ** -0.5),
        "w2": nrm(ks[26], (N_EXPERTS, D_EXPERT, D_MODEL), D_EXPERT ** -0.5),
    }


def reference(x, mem, norm_mix, w_in, q_norm, k_norm, attn_sinks, sgu_norm, sgu_w, sgu_b,
              out_norm_attn, out_norm_sgu, w_out, norm_cross, norm_mem, w_cq, w_ck, w_cv,
              cq_norm, ck_norm, w_co, norm_ffn, w_router_group, w_router_expert, w1, w3, w2):
    B, S = x.shape[:2]
    mem_n = rms_norm(mem, norm_mem)
    for _ in range(DEPTH):
        h = rms_norm(x, norm_mix)
        proj = h @ w_in
        c0 = ATTN_WIDTH
        c1 = c0 + KV_WIDTH
        c2 = c1 + KV_WIDTH
        c3 = c2 + SGU_WIDTH
        q = rms_norm(proj[..., :c0].reshape(B, S, ATTN_Q_HEADS, HEAD_DIM), q_norm)
        k = rms_norm(proj[..., c0:c1].reshape(B, S, ATTN_KV_HEADS, HEAD_DIM), k_norm)
        v = proj[..., c1:c2].reshape(B, S, ATTN_KV_HEADS, HEAD_DIM)
        u = jax.nn.gelu(proj[..., c2:c3])
        g = jax.nn.gelu(proj[..., c3:])
        attn = sliding_window_attention(q, k, v, attn_sinks)
        sg = spatial_gating(u, g, sgu_norm, sgu_w, sgu_b)
        mixed = jnp.concatenate([rms_norm(attn, out_norm_attn), rms_norm(sg, out_norm_sgu)], axis=-1)
        x = x + mixed @ w_out
        x = x + memory_cross_attention(rms_norm(x, norm_cross), mem_n, w_cq, w_ck, w_cv, cq_norm, ck_norm, w_co)
        x = x + hierarchical_moe(rms_norm(x, norm_ffn), w_router_group, w_router_expert, w1, w3, w2)
    return x
```

```python
import functools

import jax
import jax.numpy as jnp
from jax import lax
from jax.experimental import pallas as pl
from jax.experimental.pallas import tpu as pltpu

F32 = jnp.float32
BF16 = jnp.bfloat16
I32 = jnp.int32

EPS = 1e-6
HEAD_DIM = 64
ATTN_BLOCK = 128
X_HEADS = 4
N_GROUPS = 4
EXPERTS_PER_GROUP = 8
N_EXPERTS = N_GROUPS * EXPERTS_PER_GROUP
TOP_K = 2
LANES = 128
SUBLANES = 8

SEQ_TILE = 512
MOE_ROWS = 256
RANK_TILE = 1024
ROW_TILE = 256
NEG = -1e30
VMEM_LIMIT = 56 * 1024 * 1024


def _dot(a, b):
    return lax.dot_general(a, b, (((1,), (0,)), ((), ())), preferred_element_type=F32)


def _dot_nt(a, b):
    return lax.dot_general(a, b, (((1,), (1,)), ((), ())), preferred_element_type=F32)


def _rms(x, gain):
    return x * lax.rsqrt(jnp.mean(x * x, axis=-1, keepdims=True) + EPS) * gain


def _gelu(x):
    return x * (0.5 * (1.0 + jnp.tanh(0.7978845608028654 * (x + 0.044715 * (x * x * x)))))


def _group_sumsq(v, ones_blockdiag):
    sq = v * v
    hi = sq.astype(BF16)
    lo = (sq - hi.astype(F32)).astype(BF16)
    return _dot(hi, ones_blockdiag) + _dot(lo, ones_blockdiag)


def _memkv_kernel(mem_ref, nmem_ref, wck_ref, wcv_ref, k_ref, v_ref):
    d = mem_ref.shape[-1]
    hd = d // X_HEADS
    mn = _rms(mem_ref[...], nmem_ref[...]).astype(BF16)
    k = _dot(mn, wck_ref[...])
    parts = []
    for h in range(X_HEADS):
        kh = k[:, h * hd:(h + 1) * hd]
        parts.append(kh * lax.rsqrt(jnp.mean(kh * kh, axis=-1, keepdims=True) + EPS))
    k_ref[...] = jnp.concatenate(parts, axis=1).astype(BF16)
    v_ref[...] = _dot(mn, wcv_ref[...]).astype(BF16)


def _mixer_kernel(sinks_ref, x_ref, nmix_ref, win_ref, bdq_ref, bdk_ref, gq_ref, sgun_ref, sguw_ref, sgub_ref,
                  ona_ref, ons_ref, wout_ref, o_ref, kprev, vprev):
    j = pl.program_id(1)
    ts = x_ref.shape[0]
    nblk = ts // ATTN_BLOCK
    aw = gq_ref.shape[-1]
    kvw = bdk_ref.shape[-1]
    sw = sgun_ref.shape[-1]

    @pl.when(j == 0)
    def _():
        kprev[...] = jnp.zeros_like(kprev)
        vprev[...] = jnp.zeros_like(vprev)

    x = x_ref[...]
    h = _rms(x, nmix_ref[...]).astype(BF16)
    proj = _dot(h, win_ref[...])
    c0, c1, c2, c3 = aw, aw + kvw, aw + 2 * kvw, aw + 2 * kvw + sw
    q = proj[:, :c0]
    kd = proj[:, c0:c1]
    vd_b = proj[:, c1:c2].astype(BF16)
    u = _gelu(proj[:, c2:c3])
    g = _gelu(proj[:, c3:])

    qn = q * lax.rsqrt(_group_sumsq(q, bdq_ref[...]) * (1.0 / HEAD_DIM) + EPS) * gq_ref[...]
    kd_b = (kd * lax.rsqrt(_group_sumsq(kd, bdk_ref[...]) * (1.0 / HEAD_DIM) + EPS)).astype(BF16)
    lane_q = lax.broadcasted_iota(I32, (1, aw), 1)
    even_q = (lane_q & HEAD_DIM) == 0
    q_even = jnp.where(even_q, qn, 0.0).astype(BF16)
    q_odd = jnp.where(even_q, 0.0, qn).astype(BF16)

    lane = lax.broadcasted_iota(I32, (1, LANES), 1)
    even_l = lane < HEAD_DIM
    qi = lax.broadcasted_iota(I32, (ATTN_BLOCK, 2 * ATTN_BLOCK), 0)
    kj = lax.broadcasted_iota(I32, (ATTN_BLOCK, 2 * ATTN_BLOCK), 1)
    band = (kj > qi) & (kj <= qi + ATTN_BLOCK)
    first_key = jnp.where(j > 0, 0, ATTN_BLOCK)
    band_first = band & (kj >= first_key)

    n_kv = kvw // LANES
    rep = aw // (n_kv * HEAD_DIM)
    attn_rows = []
    for n in range(nblk):
        rows = slice(n * ATTN_BLOCK, (n + 1) * ATTN_BLOCK)
        prev = slice((n - 1) * ATTN_BLOCK, n * ATTN_BLOCK)
        mask = band_first if n == 0 else band
        cols = []
        for hk in range(n_kv):
            kvc = slice(hk * LANES, (hk + 1) * LANES)
            k_prev = kprev[:, kvc] if n == 0 else kd_b[prev, kvc]
            v_prev = vprev[:, kvc] if n == 0 else vd_b[prev, kvc]
            kwin = jnp.concatenate([k_prev, kd_b[rows, kvc]], axis=0)
            vwin = jnp.concatenate([v_prev, vd_b[rows, kvc]], axis=0)
            qparts = []
            for r in range(rep):
                hq = hk * rep + r
                pc = slice((hq // 2) * LANES, (hq // 2 + 1) * LANES)
                qparts.append((q_even if hq % 2 == 0 else q_odd)[rows, pc])
            s = _dot_nt(jnp.concatenate(qparts, axis=0), kwin)
            ps, dens = [], []
            for r in range(rep):
                sr = jnp.where(mask, s[r * ATTN_BLOCK:(r + 1) * ATTN_BLOCK], NEG)
                sink = sinks_ref[hk * rep + r]
                m = jnp.maximum(jnp.max(sr, axis=-1, keepdims=True), sink)
                p = jnp.exp(sr - m)
                dens.append(jnp.sum(p, axis=-1, keepdims=True) + jnp.exp(sink - m))
                ps.append(p.astype(BF16))
            o = _dot(jnp.concatenate(ps, axis=0), vwin)
            outs = [o[r * ATTN_BLOCK:(r + 1) * ATTN_BLOCK] / dens[r] for r in range(rep)]
            for r in range(0, rep, 2):
                cols.append(jnp.where(even_l, outs[r], outs[r + 1]))
        attn_rows.append(jnp.concatenate(cols, axis=1))
    attn = jnp.concatenate(attn_rows, axis=0)
    last = slice(ts - ATTN_BLOCK, ts)
    kprev[...] = kd_b[last]
    vprev[...] = vd_b[last]

    vc_b = _rms(g, sgun_ref[...]).astype(BF16)
    ri = lax.broadcasted_iota(I32, (2 * ATTN_BLOCK, ATTN_BLOCK), 0)
    ci = lax.broadcasted_iota(I32, (2 * ATTN_BLOCK, ATTN_BLOCK), 1)
    causal = (ri & (ATTN_BLOCK - 1)) >= ci
    n_pair = sw // LANES
    wst = [jnp.where(causal, sguw_ref[p], 0.0).astype(BF16) for p in range(n_pair)]
    s_rows = []
    for c in range(nblk):
        rows = slice(c * ATTN_BLOCK, (c + 1) * ATTN_BLOCK)
        cols = []
        for p in range(n_pair):
            res = _dot(wst[p], vc_b[rows, p * LANES:(p + 1) * LANES])
            cols.append(jnp.where(even_l, res[:ATTN_BLOCK], res[ATTN_BLOCK:]))
        s_rows.append(jnp.concatenate(cols, axis=1) + sgub_ref[...])
    sg = u * jnp.concatenate(s_rows, axis=0)

    an = _rms(attn, ona_ref[...]).astype(BF16)
    sn = _rms(sg, ons_ref[...]).astype(BF16)
    o_ref[...] = x + _dot(an, wout_ref[:aw, :]) + _dot(sn, wout_ref[aw:, :])


def _xattn_kernel(x_ref, ncross_ref, wcq_ref, gqk_ref, kx_ref, vx_ref, wco_ref, nffn_ref, wr1_ref, wr2_ref,
                  o_ref, ids_ref, gcol_ref):
    ts, d = x_ref.shape
    hd = d // X_HEADS
    x = x_ref[...]
    h = _rms(x, ncross_ref[...]).astype(BF16)
    q = _dot(h, wcq_ref[...])
    outs = []
    for hh in range(X_HEADS):
        hc = slice(hh * hd, (hh + 1) * hd)
        qh = q[:, hc]
        qn = (qh * lax.rsqrt(jnp.mean(qh * qh, axis=-1, keepdims=True) + EPS) * gqk_ref[...]).astype(BF16)
        s = _dot_nt(qn, kx_ref[:, hc])
        p = jnp.exp(s - jnp.max(s, axis=-1, keepdims=True))
        den = jnp.sum(p, axis=-1, keepdims=True)
        outs.append(_dot(p.astype(BF16), vx_ref[:, hc]) / den)
    x2 = x + _dot(jnp.concatenate(outs, axis=1).astype(BF16), wco_ref[...])
    o_ref[...] = x2

    h3 = _rms(x2, nffn_ref[...])
    h_hi = h3.astype(BF16)
    h_lo = (h3 - h_hi.astype(F32)).astype(BF16)
    nr = wr2_ref.shape[0]
    r1 = _dot_nt(wr1_ref[...], h_hi)
    logits = r1[:nr] + r1[nr:] + _dot_nt(wr2_ref[...], h_lo)
    ge = EXPERTS_PER_GROUP
    sub = lax.broadcasted_iota(I32, (ge, ts), 0).astype(F32)
    grp = jnp.where(sub < N_GROUPS, logits[N_EXPERTS:N_EXPERTS + ge], NEG)
    gmax = jnp.max(grp, axis=0, keepdims=True)
    gidx = jnp.min(jnp.where(grp == gmax, sub, float(ge)), axis=0, keepdims=True)
    g_w = 1.0 / jnp.sum(jnp.exp(grp - gmax), axis=0, keepdims=True)
    local = logits[(N_GROUPS - 1) * ge:N_GROUPS * ge]
    for gi in range(N_GROUPS - 2, -1, -1):
        local = jnp.where(gidx == float(gi), logits[gi * ge:(gi + 1) * ge], local)
    m1 = jnp.max(local, axis=0, keepdims=True)
    i1 = jnp.min(jnp.where(local == m1, sub, float(ge)), axis=0, keepdims=True)
    rest = jnp.where(sub == i1, NEG, local)
    m2 = jnp.max(rest, axis=0, keepdims=True)
    i2 = jnp.min(jnp.where(rest == m2, sub, float(ge)), axis=0, keepdims=True)
    e2 = jnp.exp(m2 - m1)
    den = 1.0 + e2
    w1 = g_w * (1.0 / den)
    w2 = g_w * (e2 / den)
    ids_ref[...] = jnp.where(sub == 0.0, gidx * ge + i1, jnp.where(sub == 1.0, gidx * ge + i2, 0.0)).astype(I32)
    rows = lax.broadcasted_iota(I32, (LANES, ts), 0)
    gates_t = jnp.where(rows == 0, w1, jnp.where(rows == 1, w2, 0.0))
    gcol_ref[...] = gates_t.T


def _rank_kernel(ids_ref, rank_ref, cnt_ref, carry, upper):
    i = pl.program_id(0)
    ct = ids_ref.shape[-1]

    @pl.when(i == 0)
    def _():
        carry[...] = jnp.zeros_like(carry)
        r = lax.broadcasted_iota(I32, (ct, ct), 0)
        c = lax.broadcasted_iota(I32, (ct, ct), 1)
        upper[...] = jnp.where(r < c, 1.0, 0.0).astype(BF16)

    e = ids_ref[...]
    onehot = lax.broadcasted_iota(I32, (N_EXPERTS, ct), 0) == e
    oh = jnp.where(onehot, 1.0, 0.0)
    before = _dot(oh.astype(BF16), upper[...]) + carry[:, 0:1]
    rank_ref[...] = jnp.sum(jnp.where(onehot, before, 0.0), axis=0, keepdims=True).astype(I32)
    carry[...] = carry[...] + jnp.sum(oh, axis=1, keepdims=True)

    @pl.when(i == pl.num_programs(0) - 1)
    def _():
        cnt_ref[...] = carry[...]


def _row_copy(src, dst, sem):
    return pltpu.make_async_copy(src, dst, sem)


def _dispatch_kernel(fs_ref, fl_ref, tail_ref, x_ref, nffn_ref, pos0_ref, pos1_ref, xb_ref, buf, zbuf, sem, zsem):
    i = pl.program_id(0)
    n = pl.num_programs(0)
    td = x_ref.shape[0]
    slot = i % 2
    chunk_bits = [SUBLANES << b for b in range((MOE_ROWS // SUBLANES - 1).bit_length())]

    def wait_rows(s):
        def body(r, c):
            _row_copy(buf.at[s, pl.ds(0, 1)], xb_ref.at[pl.ds(0, 1)], sem.at[s]).wait()
            _row_copy(buf.at[s, pl.ds(0, 1)], xb_ref.at[pl.ds(0, 1)], sem.at[s]).wait()
            return c
        lax.fori_loop(0, td, body, 0, unroll=8)

    def zero_fill(e, do):
        start, length = fs_ref[e], fl_ref[e]
        head = jnp.minimum((-start) & (SUBLANES - 1), length)
        for r in range(SUBLANES - 1):
            @pl.when(r < head)
            def _(r=r):
                do(_row_copy(zbuf.at[pl.ds(0, 1)], xb_ref.at[pl.ds(start + r, 1)], zsem))
        rest = length - head
        for bit in chunk_bits:
            @pl.when((rest & bit) != 0)
            def _(bit=bit):
                off = pl.multiple_of(start + head + (rest & ~(2 * bit - 1)), SUBLANES)
                do(_row_copy(zbuf.at[pl.ds(0, bit)], xb_ref.at[pl.ds(off, bit)], zsem))

    def zero_tail(do):
        def body(c, carry):
            off = pl.multiple_of(tail_ref[0] + c * zbuf.shape[0], SUBLANES)
            do(_row_copy(zbuf, xb_ref.at[pl.ds(off, zbuf.shape[0])], zsem))
            return carry
        lax.fori_loop(0, tail_ref[1], body, 0)

    def for_experts(do):
        def body(e, carry):
            zero_fill(e, do)
            return carry
        lax.fori_loop(0, N_EXPERTS, body, 0)

    @pl.when(i == 0)
    def _():
        zbuf[...] = jnp.zeros_like(zbuf)
        for_experts(lambda c: c.start())
        zero_tail(lambda c: c.start())
        for_experts(lambda c: c.wait())
        zero_tail(lambda c: c.wait())

    @pl.when(i >= 2)
    def _():
        wait_rows(slot)

    buf[slot] = _rms(x_ref[...], nffn_ref[...])

    def issue(r, c):
        src = buf.at[slot, pl.ds(r, 1)]
        _row_copy(src, xb_ref.at[pl.ds(pos0_ref[0, 0, r], 1)], sem.at[slot]).start()
        _row_copy(src, xb_ref.at[pl.ds(pos1_ref[0, 0, r], 1)], sem.at[slot]).start()
        return c
    lax.fori_loop(0, td, issue, 0, unroll=8)

    @pl.when(i == n - 1)
    def _():
        wait_rows(slot)

        @pl.when(n >= 2)
        def _():
            wait_rows(1 - slot)


def _expert_kernel(be_ref, nbu_ref, x_ref, w1_ref, w3_ref, w2_ref, y_ref):
    used = pl.program_id(0) < nbu_ref[0]

    @pl.when(used)
    def _():
        x = x_ref[...].astype(BF16)
        a = _dot(x, w1_ref[...])
        b = _dot(x, w3_ref[...])
        hid = (a * (1.0 / (1.0 + jnp.exp(-a)))) * b
        y_ref[...] = _dot(hid.astype(BF16), w2_ref[...])

    @pl.when(jnp.logical_not(used))
    def _():
        y_ref[...] = jnp.zeros_like(y_ref)


def _combine_kernel(x_ref, g_ref, pos0_ref, pos1_ref, pos0n_ref, pos1n_ref, yb_ref, o_ref, ybuf, sem):
    i = pl.program_id(0)
    n = pl.num_programs(0)
    tf = x_ref.shape[0]
    slot = i % 2

    def issue(s, p0_ref, p1_ref):
        def body(r, c):
            _row_copy(yb_ref.at[pl.ds(p0_ref[0, 0, r], 1)], ybuf.at[s, 0, pl.ds(r, 1)], sem.at[s]).start()
            _row_copy(yb_ref.at[pl.ds(p1_ref[0, 0, r], 1)], ybuf.at[s, 1, pl.ds(r, 1)], sem.at[s]).start()
            return c
        lax.fori_loop(0, tf, body, 0, unroll=8)

    @pl.when(i == 0)
    def _():
        issue(0, pos0_ref, pos1_ref)

    @pl.when(i + 1 < n)
    def _():
        issue(1 - slot, pos0n_ref, pos1n_ref)

    def wait_body(r, c):
        _row_copy(yb_ref.at[pl.ds(0, 1)], ybuf.at[slot, 0, pl.ds(0, 1)], sem.at[slot]).wait()
        _row_copy(yb_ref.at[pl.ds(0, 1)], ybuf.at[slot, 1, pl.ds(0, 1)], sem.at[slot]).wait()
        return c
    lax.fori_loop(0, tf, wait_body, 0, unroll=8)

    g = g_ref[...]
    o_ref[...] = x_ref[...] + (g[:, 0:1] * ybuf[slot, 0] + g[:, 1:2] * ybuf[slot, 1])


def _const_spec(shape):
    return pl.BlockSpec(shape, lambda *_: (0,) * len(shape))


def _blockdiag_ones(width):
    r = jnp.arange(width) // HEAD_DIM
    return (r[:, None] == r[None, :]).astype(BF16)


def kernel(x, mem, norm_mix, w_in, q_norm, k_norm, attn_sinks, sgu_norm, sgu_w, sgu_b, out_norm_attn, out_norm_sgu, w_out, norm_cross, norm_mem, w_cq, w_ck, w_cv, cq_norm, ck_norm, w_co, norm_ffn, w_router_group, w_router_expert, w1, w3, w2):
    B, S, D = x.shape
    M = mem.shape[1]
    T = B * S
    A = T * TOP_K
    n_q = attn_sinks.shape[0]
    aw = n_q * HEAD_DIM
    sw = sgu_norm.shape[0]
    n_kv = (w_in.shape[1] - aw - 2 * sw) // (2 * HEAD_DIM)
    kvw = n_kv * HEAD_DIM
    chunk = sgu_w.shape[1]
    de = w1.shape[2]
    assert chunk == ATTN_BLOCK and S % SEQ_TILE == 0 and T % ROW_TILE == 0 and A % RANK_TILE == 0
    assert n_kv * 2 * HEAD_DIM == n_kv * LANES and sgu_w.shape[0] * HEAD_DIM == sw
    row = lambda v: v.reshape(1, -1).astype(F32)
    cparams = functools.partial(pltpu.CompilerParams, vmem_limit_bytes=VMEM_LIMIT)

    dup = lambda w: jnp.repeat(w.reshape(D, n_kv, 1, HEAD_DIM), 2, axis=2).reshape(D, 2 * kvw)
    c0, c1, c2 = aw, aw + kvw, aw + 2 * kvw
    win = jnp.concatenate([w_in[:, :c0], dup(w_in[:, c0:c1]), dup(w_in[:, c1:c2]), w_in[:, c2:]], axis=1).astype(BF16)
    gq = row(jnp.tile(q_norm * k_norm * (HEAD_DIM ** -0.5), n_q))
    sguw_pairs = sgu_w.reshape(sw // LANES, 2 * chunk, chunk).astype(F32)
    sgub_full = jnp.repeat(sgu_b.T, HEAD_DIM, axis=1).astype(F32)
    hd = D // X_HEADS
    gqk = row(cq_norm * ck_norm * (hd ** -0.5))
    wr = jnp.concatenate([w_router_expert.transpose(0, 2, 1).reshape(N_EXPERTS, D), w_router_group.T,
                          jnp.zeros((EXPERTS_PER_GROUP - N_GROUPS, D), F32)], axis=0).astype(F32)
    wr_hi = wr.astype(BF16)
    wr_lo = (wr - wr_hi.astype(F32)).astype(BF16)
    wr1 = jnp.concatenate([wr_hi, wr_lo], axis=0)
    nr = wr.shape[0]

    kx, vx = pl.pallas_call(
        _memkv_kernel,
        grid=(B,),
        in_specs=[pl.BlockSpec((None, M, D), lambda b: (b, 0, 0)), _const_spec((1, D)), _const_spec((D, D)),
                  _const_spec((D, D))],
        out_specs=[pl.BlockSpec((None, M, D), lambda b: (b, 0, 0))] * 2,
        out_shape=[jax.ShapeDtypeStruct((B, M, D), BF16)] * 2,
        compiler_params=cparams(dimension_semantics=("arbitrary",)),
        name="memkv",
    )(mem, row(norm_mem), w_ck.astype(BF16), w_cv.astype(BF16))

    nseq = S // SEQ_TILE
    xtile = pl.BlockSpec((None, SEQ_TILE, D), lambda b, j: (b, j, 0))
    x1 = pl.pallas_call(
        _mixer_kernel,
        grid=(B, nseq),
        in_specs=[pl.BlockSpec(memory_space=pltpu.SMEM), xtile, _const_spec((1, D)), _const_spec(win.shape),
                  _const_spec((aw, aw)), _const_spec((2 * kvw, 2 * kvw)), _const_spec((1, aw)), _const_spec((1, sw)),
                  _const_spec(sguw_pairs.shape), _const_spec(sgub_full.shape), _const_spec((1, aw)),
                  _const_spec((1, sw)), _const_spec((D, D))],
        out_specs=xtile,
        out_shape=jax.ShapeDtypeStruct((B, S, D), F32),
        scratch_shapes=[pltpu.VMEM((ATTN_BLOCK, 2 * kvw), BF16), pltpu.VMEM((ATTN_BLOCK, 2 * kvw), BF16)],
        compiler_params=cparams(dimension_semantics=("arbitrary", "arbitrary")),
        name="mixer",
    )(attn_sinks.astype(F32), x, row(norm_mix), win, _blockdiag_ones(aw), _blockdiag_ones(2 * kvw), gq,
      row(sgu_norm), sguw_pairs, sgub_full, row(out_norm_attn), row(out_norm_sgu), w_out.astype(BF16))

    x2, ids, gcol = pl.pallas_call(
        _xattn_kernel,
        grid=(B, nseq),
        in_specs=[xtile, _const_spec((1, D)), _const_spec((D, D)), _const_spec((1, hd)),
                  pl.BlockSpec((None, M, D), lambda b, j: (b, 0, 0)), pl.BlockSpec((None, M, D), lambda b, j: (b, 0, 0)),
                  _const_spec((D, D)), _const_spec((1, D)), _const_spec((2 * nr, D)), _const_spec((nr, D))],
        out_specs=[xtile, pl.BlockSpec((EXPERTS_PER_GROUP, SEQ_TILE), lambda b, j: (0, b * nseq + j)),
                   pl.BlockSpec((SEQ_TILE, LANES), lambda b, j: (b * nseq + j, 0))],
        out_shape=[jax.ShapeDtypeStruct((B, S, D), F32), jax.ShapeDtypeStruct((EXPERTS_PER_GROUP, T), I32),
                   jax.ShapeDtypeStruct((T, LANES), F32)],
        compiler_params=cparams(dimension_semantics=("arbitrary", "arbitrary")),
        name="xattn",
    )(x1, row(norm_cross), w_cq.astype(BF16), gqk, kx, vx, w_co.astype(BF16), row(norm_ffn), wr1, wr_hi)
    x2 = x2.reshape(T, D)
    ids = ids[:TOP_K]

    rank, cnt = pl.pallas_call(
        _rank_kernel,
        grid=(A // RANK_TILE,),
        in_specs=[pl.BlockSpec((1, RANK_TILE), lambda i: (0, i))],
        out_specs=[pl.BlockSpec((1, RANK_TILE), lambda i: (0, i)), _const_spec((N_EXPERTS, LANES))],
        out_shape=[jax.ShapeDtypeStruct((1, A), I32), jax.ShapeDtypeStruct((N_EXPERTS, LANES), F32)],
        scratch_shapes=[pltpu.VMEM((N_EXPERTS, LANES), F32), pltpu.VMEM((RANK_TILE, RANK_TILE), BF16)],
        compiler_params=cparams(dimension_semantics=("arbitrary",)),
        name="rank",
    )(ids.reshape(1, A))

    counts = cnt[:, 0].astype(I32)
    padded = (counts + MOE_ROWS - 1) // MOE_ROWS * MOE_ROWS
    pad_end = jnp.cumsum(padded)
    pad_start = pad_end - padded
    nb = A // MOE_ROWS + N_EXPERTS
    nbu = (pad_end[-1:] // MOE_ROWS).astype(I32)
    blk = jnp.minimum(jnp.arange(nb, dtype=I32), nbu[0] - 1)
    block_e = jnp.minimum(jnp.searchsorted(pad_end, blk * MOE_ROWS, side="right"), N_EXPERTS - 1).astype(I32)
    pos = (jnp.take(pad_start, ids.reshape(-1)) + rank.reshape(-1)).reshape(TOP_K, T // ROW_TILE, 1, ROW_TILE)
    fill_start = (pad_start + counts).astype(I32)
    fill_len = (padded - counts).astype(I32)
    zrows = MOE_ROWS // 2
    tail = jnp.concatenate([pad_end[-1:], (nb - nbu) * (MOE_ROWS // zrows)]).astype(I32)

    nrow = T // ROW_TILE
    P = nb * MOE_ROWS
    pos_spec = pl.BlockSpec((1, 1, ROW_TILE), lambda i, *_: (i, 0, 0), memory_space=pltpu.SMEM)
    xb = pl.pallas_call(
        _dispatch_kernel,
        grid_spec=pltpu.PrefetchScalarGridSpec(
            num_scalar_prefetch=3,
            grid=(nrow,),
            in_specs=[pl.BlockSpec((ROW_TILE, D), lambda i, *_: (i, 0)), pl.BlockSpec((1, D), lambda i, *_: (0, 0)),
                      pos_spec, pos_spec],
            out_specs=pl.BlockSpec(memory_space=pl.ANY),
            scratch_shapes=[pltpu.VMEM((2, ROW_TILE, D), F32), pltpu.VMEM((zrows, D), F32),
                            pltpu.SemaphoreType.DMA((2,)), pltpu.SemaphoreType.DMA(())],
        ),
        out_shape=jax.ShapeDtypeStruct((P, D), F32),
        compiler_params=cparams(dimension_semantics=("arbitrary",)),
        name="dispatch",
    )(fill_start, fill_len, tail, x2, row(norm_ffn), pos[0], pos[1])

    def xmap(i, be, nbu_):
        return (jnp.minimum(i, nbu_[0] - 1), 0)

    def wmap(i, be, nbu_):
        return (be[i], 0, 0)

    yb = pl.pallas_call(
        _expert_kernel,
        grid_spec=pltpu.PrefetchScalarGridSpec(
            num_scalar_prefetch=2,
            grid=(nb,),
            in_specs=[pl.BlockSpec((MOE_ROWS, D), xmap), pl.BlockSpec((None, D, de), wmap),
                      pl.BlockSpec((None, D, de), wmap), pl.BlockSpec((None, de, D), wmap)],
            out_specs=pl.BlockSpec((MOE_ROWS, D), lambda i, *_: (i, 0)),
        ),
        out_shape=jax.ShapeDtypeStruct((P, D), F32),
        compiler_params=cparams(dimension_semantics=("arbitrary",)),
        name="experts",
    )(block_e, nbu, xb, w1.astype(BF16), w3.astype(BF16), w2.astype(BF16))

    pos_next = pl.BlockSpec((1, 1, ROW_TILE), lambda i: (jnp.minimum(i + 1, nrow - 1), 0, 0), memory_space=pltpu.SMEM)
    pos_cur = pl.BlockSpec((1, 1, ROW_TILE), lambda i: (i, 0, 0), memory_space=pltpu.SMEM)
    out = pl.pallas_call(
        _combine_kernel,
        grid=(nrow,),
        in_specs=[pl.BlockSpec((ROW_TILE, D), lambda i: (i, 0)), pl.BlockSpec((ROW_TILE, LANES), lambda i: (i, 0)),
                  pos_cur, pos_cur, pos_next, pos_next, pl.BlockSpec(memory_space=pl.ANY)],
        out_specs=pl.BlockSpec((ROW_TILE, D), lambda i: (i, 0)),
        out_shape=jax.ShapeDtypeStruct((T, D), F32),
        scratch_shapes=[pltpu.VMEM((2, TOP_K, ROW_TILE, D), F32), pltpu.SemaphoreType.DMA((2,))],
        compiler_params=cparams(dimension_semantics=("arbitrary",)),
        name="combine",
    )(x2, gcol, pos[0], pos[1], pos[0], pos[1], yb)
    return out.reshape(B, S, D)
```

```python
import functools

import jax
import jax.numpy as jnp
from jax import lax
from jax.experimental import pallas as pl
from jax.experimental.pallas import tpu as pltpu

F32 = jnp.float32
BF16 = jnp.bfloat16
I32 = jnp.int32

EPS = 1e-6
HEAD_DIM = 64
ATTN_BLOCK = 128
X_HEADS = 4
N_GROUPS = 4
EXPERTS_PER_GROUP = 8
N_EXPERTS = N_GROUPS * EXPERTS_PER_GROUP
TOP_K = 2
LANES = 128
SUBLANES = 8

SEQ_TILE = 512
MOE_ROWS = 256
RANK_TILE = 1024
ROW_TILE = 256
NEG = -1e30
VMEM_LIMIT = 56 * 1024 * 1024


def _dot(a, b):
    return lax.dot_general(a, b, (((1,), (0,)), ((), ())), preferred_element_type=F32)


def _dot_nt(a, b):
    return lax.dot_general(a, b, (((1,), (1,)), ((), ())), preferred_element_type=F32)


def _rms(x, gain):
    return x * lax.rsqrt(jnp.mean(x * x, axis=-1, keepdims=True) + EPS) * gain


def _gelu(x):
    return x * (0.5 * (1.0 + jnp.tanh(0.7978845608028654 * (x + 0.044715 * (x * x * x)))))


def _group_sumsq(v, ones_blockdiag):
    sq = v * v
    hi = sq.astype(BF16)
    lo = (sq - hi.astype(F32)).astype(BF16)
    return _dot(hi, ones_blockdiag) + _dot(lo, ones_blockdiag)


def _memkv_kernel(mem_ref, nmem_ref, wck_ref, wcv_ref, k_ref, v_ref):
    d = mem_ref.shape[-1]
    hd = d // X_HEADS
    mn = _rms(mem_ref[...], nmem_ref[...]).astype(BF16)
    k = _dot(mn, wck_ref[...])
    parts = []
    for h in range(X_HEADS):
        kh = k[:, h * hd:(h + 1) * hd]
        parts.append(kh * lax.rsqrt(jnp.mean(kh * kh, axis=-1, keepdims=True) + EPS))
    k_ref[...] = jnp.concatenate(parts, axis=1).astype(BF16)
    v_ref[...] = _dot(mn, wcv_ref[...]).astype(BF16)


def _mixer_kernel(sinks_ref, x_ref, nmix_ref, win_ref, bdq_ref, bdk_ref, gq_ref, sgun_ref, sguw_ref, sgub_ref,
                  ona_ref, ons_ref, wout_ref, o_ref, kprev, vprev):
    j = pl.program_id(1)
    ts = x_ref.shape[0]
    nblk = ts // ATTN_BLOCK
    aw = gq_ref.shape[-1]
    kvw = bdk_ref.shape[-1]
    sw = sgun_ref.shape[-1]

    @pl.when(j == 0)
    def _():
        kprev[...] = jnp.zeros_like(kprev)
        vprev[...] = jnp.zeros_like(vprev)

    x = x_ref[...]
    h = _rms(x, nmix_ref[...]).astype(BF16)
    proj = _dot(h, win_ref[...])
    c0, c1, c2, c3 = aw, aw + kvw, aw + 2 * kvw, aw + 2 * kvw + sw
    q = proj[:, :c0]
    kd = proj[:, c0:c1]
    vd_b = proj[:, c1:c2].astype(BF16)
    u = _gelu(proj[:, c2:c3])
    g = _gelu(proj[:, c3:])

    qn = q * lax.rsqrt(_group_sumsq(q, bdq_ref[...]) * (1.0 / HEAD_DIM) + EPS) * gq_ref[...]
    kd_b = (kd * lax.rsqrt(_group_sumsq(kd, bdk_ref[...]) * (1.0 / HEAD_DIM) + EPS)).astype(BF16)
    lane_q = lax.broadcasted_iota(I32, (1, aw), 1)
    even_q = (lane_q & HEAD_DIM) == 0
    q_even = jnp.where(even_q, qn, 0.0).astype(BF16)
    q_odd = jnp.where(even_q, 0.0, qn).astype(BF16)

    lane = lax.broadcasted_iota(I32, (1, LANES), 1)
    even_l = lane < HEAD_DIM
    qi = lax.broadcasted_iota(I32, (ATTN_BLOCK, 2 * ATTN_BLOCK), 0)
    kj = lax.broadcasted_iota(I32, (ATTN_BLOCK, 2 * ATTN_BLOCK), 1)
    band = (kj > qi) & (kj <= qi + ATTN_BLOCK)
    first_key = jnp.where(j > 0, 0, ATTN_BLOCK)
    band_first = band & (kj >= first_key)

    n_kv = kvw // LANES
    rep = aw // (n_kv * HEAD_DIM)
    attn_rows = []
    for n in range(nblk):
        rows = slice(n * ATTN_BLOCK, (n + 1) * ATTN_BLOCK)
        prev = slice((n - 1) * ATTN_BLOCK, n * ATTN_BLOCK)
        mask = band_first if n == 0 else band
        cols = []
        for hk in range(n_kv):
            kvc = slice(hk * LANES, (hk + 1) * LANES)
            k_prev = kprev[:, kvc] if n == 0 else kd_b[prev, kvc]
            v_prev = vprev[:, kvc] if n == 0 else vd_b[prev, kvc]
            kwin = jnp.concatenate([k_prev, kd_b[rows, kvc]], axis=0)
            vwin = jnp.concatenate([v_prev, vd_b[rows, kvc]], axis=0)
            qparts = []
            for r in range(rep):
                hq = hk * rep + r
                pc = slice((hq // 2) * LANES, (hq // 2 + 1) * LANES)
                qparts.append((q_even if hq % 2 == 0 else q_odd)[rows, pc])
            s = _dot_nt(jnp.concatenate(qparts, axis=0), kwin)
            ps, dens = [], []
            for r in range(rep):
                sr = jnp.where(mask, s[r * ATTN_BLOCK:(r + 1) * ATTN_BLOCK], NEG)
                sink = sinks_ref[hk * rep + r]
                m = jnp.maximum(jnp.max(sr, axis=-1, keepdims=True), sink)
                p = jnp.exp(sr - m)
                dens.append(jnp.sum(p, axis=-1, keepdims=True) + jnp.exp(sink - m))
                ps.append(p.astype(BF16))
            o = _dot(jnp.concatenate(ps, axis=0), vwin)
            outs = [o[r * ATTN_BLOCK:(r + 1) * ATTN_BLOCK] / dens[r] for r in range(rep)]
            for r in range(0, rep, 2):
                cols.append(jnp.where(even_l, outs[r], outs[r + 1]))
        attn_rows.append(jnp.concatenate(cols, axis=1))
    attn = jnp.concatenate(attn_rows, axis=0)
    last = slice(ts - ATTN_BLOCK, ts)
    kprev[...] = kd_b[last]
    vprev[...] = vd_b[last]

    vc_b = _rms(g, sgun_ref[...]).astype(BF16)
    ri = lax.broadcasted_iota(I32, (2 * ATTN_BLOCK, ATTN_BLOCK), 0)
    ci = lax.broadcasted_iota(I32, (2 * ATTN_BLOCK, ATTN_BLOCK), 1)
    causal = (ri & (ATTN_BLOCK - 1)) >= ci
    n_pair = sw // LANES
    wst = [jnp.where(causal, sguw_ref[p], 0.0).astype(BF16) for p in range(n_pair)]
    s_rows = []
    for c in range(nblk):
        rows = slice(c * ATTN_BLOCK, (c + 1) * ATTN_BLOCK)
        cols = []
        for p in range(n_pair):
            res = _dot(wst[p], vc_b[rows, p * LANES:(p + 1) * LANES])
            cols.append(jnp.where(even_l, res[:ATTN_BLOCK], res[ATTN_BLOCK:]))
        s_rows.append(jnp.concatenate(cols, axis=1) + sgub_ref[...])
    sg = u * jnp.concatenate(s_rows, axis=0)

    an = _rms(attn, ona_ref[...]).astype(BF16)
    sn = _rms(sg, ons_ref[...]).astype(BF16)
    o_ref[...] = x + _dot(an, wout_ref[:aw, :]) + _dot(sn, wout_ref[aw:, :])


def _xattn_kernel(x_ref, ncross_ref, wcq_ref, gqk_ref, kx_ref, vx_ref, wco_ref, nffn_ref, wr1_ref, wr2_ref,
                  o_ref, ids_ref, gcol_ref):
    ts, d = x_ref.shape
    hd = d // X_HEADS
    x = x_ref[...]
    h = _rms(x, ncross_ref[...]).astype(BF16)
    q = _dot(h, wcq_ref[...])
    outs = []
    for hh in range(X_HEADS):
        hc = slice(hh * hd, (hh + 1) * hd)
        qh = q[:, hc]
        qn = (qh * lax.rsqrt(jnp.mean(qh * qh, axis=-1, keepdims=True) + EPS) * gqk_ref[...]).astype(BF16)
        s = _dot_nt(qn, kx_ref[:, hc])
        p = jnp.exp(s - jnp.max(s, axis=-1, keepdims=True))
        den = jnp.sum(p, axis=-1, keepdims=True)
        outs.append(_dot(p.astype(BF16), vx_ref[:, hc]) / den)
    x2 = x + _dot(jnp.concatenate(outs, axis=1).astype(BF16), wco_ref[...])
    o_ref[...] = x2

    h3 = _rms(x2, nffn_ref[...])
    h_hi = h3.astype(BF16)
    h_lo = (h3 - h_hi.astype(F32)).astype(BF16)
    nr = wr2_ref.shape[0]
    r1 = _dot_nt(wr1_ref[...], h_hi)
    logits = r1[:nr] + r1[nr:] + _dot_nt(wr2_ref[...], h_lo)
    ge = EXPERTS_PER_GROUP
    sub = lax.broadcasted_iota(I32, (ge, ts), 0).astype(F32)
    grp = jnp.where(sub < N_GROUPS, logits[N_EXPERTS:N_EXPERTS + ge], NEG)
    gmax = jnp.max(grp, axis=0, keepdims=True)
    gidx = jnp.min(jnp.where(grp == gmax, sub, float(ge)), axis=0, keepdims=True)
    g_w = 1.0 / jnp.sum(jnp.exp(grp - gmax), axis=0, keepdims=True)
    local = logits[(N_GROUPS - 1) * ge:N_GROUPS * ge]
    for gi in range(N_GROUPS - 2, -1, -1):
        local = jnp.where(gidx == float(gi), logits[gi * ge:(gi + 1) * ge], local)
    m1 = jnp.max(local, axis=0, keepdims=True)
    i1 = jnp.min(jnp.where(local == m1, sub, float(ge)), axis=0, keepdims=True)
    rest = jnp.where(sub == i1, NEG, local)
    m2 = jnp.max(rest, axis=0, keepdims=True)
    i2 = jnp.min(jnp.where(rest == m2, sub, float(ge)), axis=0, keepdims=True)
    e2 = jnp.exp(m2 - m1)
    den = 1.0 + e2
    w1 = g_w * (1.0 / den)
    w2 = g_w * (e2 / den)
    ids_ref[...] = jnp.where(sub == 0.0, gidx * ge + i1, jnp.where(sub == 1.0, gidx * ge + i2, 0.0)).astype(I32)
    rows = lax.broadcasted_iota(I32, (LANES, ts), 0)
    gates_t = jnp.where(rows == 0, w1, jnp.where(rows == 1, w2, 0.0))
    gcol_ref[...] = gates_t.T


def _rank_kernel(ids_ref, rank_ref, cnt_ref, carry, upper):
    i = pl.program_id(0)
    ct = ids_ref.shape[-1]

    @pl.when(i == 0)
    def _():
        carry[...] = jnp.zeros_like(carry)
        r = lax.broadcasted_iota(I32, (ct, ct), 0)
        c = lax.broadcasted_iota(I32, (ct, ct), 1)
        upper[...] = jnp.where(r < c, 1.0, 0.0).astype(BF16)

    e = ids_ref[...]
    onehot = lax.broadcasted_iota(I32, (N_EXPERTS, ct), 0) == e
    oh = jnp.where(onehot, 1.0, 0.0)
    before = _dot(oh.astype(BF16), upper[...]) + carry[:, 0:1]
    rank_ref[...] = jnp.sum(jnp.where(onehot, before, 0.0), axis=0, keepdims=True).astype(I32)
    carry[...] = carry[...] + jnp.sum(oh, axis=1, keepdims=True)

    @pl.when(i == pl.num_programs(0) - 1)
    def _():
        cnt_ref[...] = carry[...]


def _row_copy(src, dst, sem):
    return pltpu.make_async_copy(src, dst, sem)


def _dispatch_kernel(fs_ref, fl_ref, tail_ref, x_ref, nffn_ref, pos0_ref, pos1_ref, xb_ref, buf, zbuf, sem, zsem):
    i = pl.program_id(0)
    n = pl.num_programs(0)
    td = x_ref.shape[0]
    slot = i % 2
    zrows = zbuf.shape[0] // SUBLANES
    chunk_bits = [1 << b for b in range((MOE_ROWS - 1).bit_length())]

    def tile_rows(ref, first, count=1):
        return ref.at[pl.ds(pl.multiple_of(first * SUBLANES, SUBLANES), count * SUBLANES)]

    def wait_rows(s):
        def body(r, c):
            _row_copy(tile_rows(buf.at[s], 0), tile_rows(xb_ref, 0), sem.at[s]).wait()
            _row_copy(tile_rows(buf.at[s], 0), tile_rows(xb_ref, 0), sem.at[s]).wait()
            return c
        lax.fori_loop(0, td, body, 0, unroll=8)

    def zero_fill(e, do):
        start, length = fs_ref[e], fl_ref[e]
        for bit in chunk_bits:
            @pl.when((length & bit) != 0)
            def _(bit=bit):
                off = start + (length & ~(2 * bit - 1))
                do(_row_copy(tile_rows(zbuf, 0, bit), tile_rows(xb_ref, off, bit), zsem))

    def zero_tail(do):
        def body(c, carry):
            do(_row_copy(zbuf, tile_rows(xb_ref, tail_ref[0] + c * zrows, zrows), zsem))
            return carry
        lax.fori_loop(0, tail_ref[1], body, 0)

    def for_experts(do):
        def body(e, carry):
            zero_fill(e, do)
            return carry
        lax.fori_loop(0, N_EXPERTS, body, 0)

    @pl.when(i == 0)
    def _():
        zbuf[...] = jnp.zeros_like(zbuf)
        for_experts(lambda c: c.start())
        zero_tail(lambda c: c.start())
        for_experts(lambda c: c.wait())
        zero_tail(lambda c: c.wait())

    @pl.when(i >= 2)
    def _():
        wait_rows(slot)

    h3 = _rms(x_ref[...], nffn_ref[...])
    for c in range(SUBLANES):
        buf[slot, pl.ds(c, td, stride=SUBLANES), :] = h3[:, c * LANES:(c + 1) * LANES]

    def issue(r, c):
        src = tile_rows(buf.at[slot], r)
        _row_copy(src, tile_rows(xb_ref, pos0_ref[0, 0, r]), sem.at[slot]).start(priority=0)
        _row_copy(src, tile_rows(xb_ref, pos1_ref[0, 0, r]), sem.at[slot]).start(priority=1)
        return c
    lax.fori_loop(0, td, issue, 0, unroll=8)

    @pl.when(i == n - 1)
    def _():
        wait_rows(slot)

        @pl.when(n >= 2)
        def _():
            wait_rows(1 - slot)


def _expert_kernel(be_ref, nbu_ref, x_ref, w1_ref, w3_ref, w2_ref, y_ref, w1b, w3b, w2b):
    i = pl.program_id(0)
    used = i < nbu_ref[0]

    @pl.when(jnp.logical_or(i == 0, be_ref[i] != be_ref[jnp.maximum(i - 1, 0)]))
    def _():
        w1b[...] = w1_ref[...].astype(BF16)
        w3b[...] = w3_ref[...].astype(BF16)
        w2b[...] = w2_ref[...].astype(BF16)

    @pl.when(used)
    def _():
        rows = x_ref.shape[0] // SUBLANES
        x = jnp.concatenate([x_ref[pl.ds(c, rows, stride=SUBLANES), :] for c in range(SUBLANES)], axis=1)
        x = x.astype(BF16)
        a = _dot(x, w1b[...])
        b = _dot(x, w3b[...])
        hid = (a * (1.0 / (1.0 + jnp.exp(-a)))) * b
        y = _dot(hid.astype(BF16), w2b[...])
        for c in range(SUBLANES):
            y_ref[pl.ds(c, rows, stride=SUBLANES), :] = y[:, c * LANES:(c + 1) * LANES]

    @pl.when(jnp.logical_not(used))
    def _():
        y_ref[...] = jnp.zeros_like(y_ref)


def _combine_kernel(x_ref, g_ref, pos0_ref, pos1_ref, pos0n_ref, pos1n_ref, yb_ref, o_ref, ybuf, sem):
    i = pl.program_id(0)
    n = pl.num_programs(0)
    tf = x_ref.shape[0]
    slot = i % 2

    def tile_rows(ref, first):
        return ref.at[pl.ds(pl.multiple_of(first * SUBLANES, SUBLANES), SUBLANES)]

    def issue(s, p0_ref, p1_ref):
        def body(r, c):
            _row_copy(tile_rows(yb_ref, p0_ref[0, 0, r]), tile_rows(ybuf.at[s, 0], r), sem.at[s]).start(priority=0)
            _row_copy(tile_rows(yb_ref, p1_ref[0, 0, r]), tile_rows(ybuf.at[s, 1], r), sem.at[s]).start(priority=1)
            return c
        lax.fori_loop(0, tf, body, 0, unroll=8)

    @pl.when(i == 0)
    def _():
        issue(0, pos0_ref, pos1_ref)

    @pl.when(i + 1 < n)
    def _():
        issue(1 - slot, pos0n_ref, pos1n_ref)

    def wait_body(r, c):
        _row_copy(tile_rows(yb_ref, 0), tile_rows(ybuf.at[slot, 0], 0), sem.at[slot]).wait()
        _row_copy(tile_rows(yb_ref, 0), tile_rows(ybuf.at[slot, 1], 0), sem.at[slot]).wait()
        return c
    lax.fori_loop(0, tf, wait_body, 0, unroll=8)

    def rows_of(k):
        return jnp.concatenate([ybuf[slot, k, pl.ds(c, tf, stride=SUBLANES), :] for c in range(SUBLANES)], axis=1)

    g = g_ref[...]
    o_ref[...] = x_ref[...] + (g[:, 0:1] * rows_of(0) + g[:, 1:2] * rows_of(1))


def _const_spec(shape):
    return pl.BlockSpec(shape, lambda *_: (0,) * len(shape))


def _blockdiag_ones(width):
    r = jnp.arange(width) // HEAD_DIM
    return (r[:, None] == r[None, :]).astype(BF16)


def kernel(x, mem, norm_mix, w_in, q_norm, k_norm, attn_sinks, sgu_norm, sgu_w, sgu_b, out_norm_attn, out_norm_sgu, w_out, norm_cross, norm_mem, w_cq, w_ck, w_cv, cq_norm, ck_norm, w_co, norm_ffn, w_router_group, w_router_expert, w1, w3, w2):
    B, S, D = x.shape
    M = mem.shape[1]
    T = B * S
    A = T * TOP_K
    n_q = attn_sinks.shape[0]
    aw = n_q * HEAD_DIM
    sw = sgu_norm.shape[0]
    n_kv = (w_in.shape[1] - aw - 2 * sw) // (2 * HEAD_DIM)
    kvw = n_kv * HEAD_DIM
    chunk = sgu_w.shape[1]
    de = w1.shape[2]
    assert chunk == ATTN_BLOCK and S % SEQ_TILE == 0 and T % ROW_TILE == 0 and A % RANK_TILE == 0
    assert n_kv * 2 * HEAD_DIM == n_kv * LANES and sgu_w.shape[0] * HEAD_DIM == sw
    assert D == SUBLANES * LANES
    row = lambda v: v.reshape(1, -1).astype(F32)
    cparams = functools.partial(pltpu.CompilerParams, vmem_limit_bytes=VMEM_LIMIT)

    dup = lambda w: jnp.repeat(w.reshape(D, n_kv, 1, HEAD_DIM), 2, axis=2).reshape(D, 2 * kvw)
    c0, c1, c2 = aw, aw + kvw, aw + 2 * kvw
    win = jnp.concatenate([w_in[:, :c0], dup(w_in[:, c0:c1]), dup(w_in[:, c1:c2]), w_in[:, c2:]], axis=1).astype(BF16)
    gq = row(jnp.tile(q_norm * k_norm * (HEAD_DIM ** -0.5), n_q))
    sguw_pairs = sgu_w.reshape(sw // LANES, 2 * chunk, chunk).astype(F32)
    sgub_full = jnp.repeat(sgu_b.T, HEAD_DIM, axis=1).astype(F32)
    hd = D // X_HEADS
    gqk = row(cq_norm * ck_norm * (hd ** -0.5))
    wr = jnp.concatenate([w_router_expert.transpose(0, 2, 1).reshape(N_EXPERTS, D), w_router_group.T,
                          jnp.zeros((EXPERTS_PER_GROUP - N_GROUPS, D), F32)], axis=0).astype(F32)
    wr_hi = wr.astype(BF16)
    wr_lo = (wr - wr_hi.astype(F32)).astype(BF16)
    wr1 = jnp.concatenate([wr_hi, wr_lo], axis=0)
    nr = wr.shape[0]

    kx, vx = pl.pallas_call(
        _memkv_kernel,
        grid=(B,),
        in_specs=[pl.BlockSpec((None, M, D), lambda b: (b, 0, 0)), _const_spec((1, D)), _const_spec((D, D)),
                  _const_spec((D, D))],
        out_specs=[pl.BlockSpec((None, M, D), lambda b: (b, 0, 0))] * 2,
        out_shape=[jax.ShapeDtypeStruct((B, M, D), BF16)] * 2,
        compiler_params=cparams(dimension_semantics=("arbitrary",)),
        name="memkv",
    )(mem, row(norm_mem), w_ck.astype(BF16), w_cv.astype(BF16))

    nseq = S // SEQ_TILE
    xtile = pl.BlockSpec((None, SEQ_TILE, D), lambda b, j: (b, j, 0))
    x1 = pl.pallas_call(
        _mixer_kernel,
        grid=(B, nseq),
        in_specs=[pl.BlockSpec(memory_space=pltpu.SMEM), xtile, _const_spec((1, D)), _const_spec(win.shape),
                  _const_spec((aw, aw)), _const_spec((2 * kvw, 2 * kvw)), _const_spec((1, aw)), _const_spec((1, sw)),
                  _const_spec(sguw_pairs.shape), _const_spec(sgub_full.shape), _const_spec((1, aw)),
                  _const_spec((1, sw)), _const_spec((D, D))],
        out_specs=xtile,
        out_shape=jax.ShapeDtypeStruct((B, S, D), F32),
        scratch_shapes=[pltpu.VMEM((ATTN_BLOCK, 2 * kvw), BF16), pltpu.VMEM((ATTN_BLOCK, 2 * kvw), BF16)],
        compiler_params=cparams(dimension_semantics=("arbitrary", "arbitrary")),
        name="mixer",
    )(attn_sinks.astype(F32), x, row(norm_mix), win, _blockdiag_ones(aw), _blockdiag_ones(2 * kvw), gq,
      row(sgu_norm), sguw_pairs, sgub_full, row(out_norm_attn), row(out_norm_sgu), w_out.astype(BF16))

    x2, ids, gcol = pl.pallas_call(
        _xattn_kernel,
        grid=(B, nseq),
        in_specs=[xtile, _const_spec((1, D)), _const_spec((D, D)), _const_spec((1, hd)),
                  pl.BlockSpec((None, M, D), lambda b, j: (b, 0, 0)), pl.BlockSpec((None, M, D), lambda b, j: (b, 0, 0)),
                  _const_spec((D, D)), _const_spec((1, D)), _const_spec((2 * nr, D)), _const_spec((nr, D))],
        out_specs=[xtile, pl.BlockSpec((EXPERTS_PER_GROUP, SEQ_TILE), lambda b, j: (0, b * nseq + j)),
                   pl.BlockSpec((SEQ_TILE, LANES), lambda b, j: (b * nseq + j, 0))],
        out_shape=[jax.ShapeDtypeStruct((B, S, D), F32), jax.ShapeDtypeStruct((EXPERTS_PER_GROUP, T), I32),
                   jax.ShapeDtypeStruct((T, LANES), F32)],
        compiler_params=cparams(dimension_semantics=("arbitrary", "arbitrary")),
        name="xattn",
    )(x1, row(norm_cross), w_cq.astype(BF16), gqk, kx, vx, w_co.astype(BF16), row(norm_ffn), wr1, wr_hi)
    x2 = x2.reshape(T, D)
    ids = ids[:TOP_K]

    rank, cnt = pl.pallas_call(
        _rank_kernel,
        grid=(A // RANK_TILE,),
        in_specs=[pl.BlockSpec((1, RANK_TILE), lambda i: (0, i))],
        out_specs=[pl.BlockSpec((1, RANK_TILE), lambda i: (0, i)), _const_spec((N_EXPERTS, LANES))],
        out_shape=[jax.ShapeDtypeStruct((1, A), I32), jax.ShapeDtypeStruct((N_EXPERTS, LANES), F32)],
        scratch_shapes=[pltpu.VMEM((N_EXPERTS, LANES), F32), pltpu.VMEM((RANK_TILE, RANK_TILE), BF16)],
        compiler_params=cparams(dimension_semantics=("arbitrary",)),
        name="rank",
    )(ids.reshape(1, A))

    counts = cnt[:, 0].astype(I32)
    padded = (counts + MOE_ROWS - 1) // MOE_ROWS * MOE_ROWS
    pad_end = jnp.cumsum(padded)
    pad_start = pad_end - padded
    nb = A // MOE_ROWS + N_EXPERTS
    nbu = (pad_end[-1:] // MOE_ROWS).astype(I32)
    blk = jnp.minimum(jnp.arange(nb, dtype=I32), nbu[0] - 1)
    block_e = jnp.minimum(jnp.sum(pad_end[None, :] <= blk[:, None] * MOE_ROWS, axis=1), N_EXPERTS - 1).astype(I32)
    pos = (jnp.take(pad_start, ids.reshape(-1)) + rank.reshape(-1)).reshape(TOP_K, T // ROW_TILE, 1, ROW_TILE)
    fill_start = (pad_start + counts).astype(I32)
    fill_len = (padded - counts).astype(I32)
    zrows = MOE_ROWS // 2
    tail = jnp.concatenate([pad_end[-1:], (nb - nbu) * (MOE_ROWS // zrows)]).astype(I32)

    nrow = T // ROW_TILE
    P = nb * MOE_ROWS
    pos_spec = pl.BlockSpec((1, 1, ROW_TILE), lambda i, *_: (i, 0, 0), memory_space=pltpu.SMEM)
    xb = pl.pallas_call(
        _dispatch_kernel,
        grid_spec=pltpu.PrefetchScalarGridSpec(
            num_scalar_prefetch=3,
            grid=(nrow,),
            in_specs=[pl.BlockSpec((ROW_TILE, D), lambda i, *_: (i, 0)), pl.BlockSpec((1, D), lambda i, *_: (0, 0)),
                      pos_spec, pos_spec],
            out_specs=pl.BlockSpec(memory_space=pl.ANY),
            scratch_shapes=[pltpu.VMEM((2, ROW_TILE * SUBLANES, LANES), F32), pltpu.VMEM((zrows * SUBLANES, LANES), F32),
                            pltpu.SemaphoreType.DMA((2,)), pltpu.SemaphoreType.DMA(())],
        ),
        out_shape=jax.ShapeDtypeStruct((P * SUBLANES, LANES), F32),
        compiler_params=cparams(dimension_semantics=("arbitrary",)),
        name="dispatch",
    )(fill_start, fill_len, tail, x2, row(norm_ffn), pos[0], pos[1])

    def xmap(i, be, nbu_):
        return (jnp.minimum(i, nbu_[0] - 1), 0)

    def wmap(i, be, nbu_):
        return (be[i], 0, 0)

    yb = pl.pallas_call(
        _expert_kernel,
        grid_spec=pltpu.PrefetchScalarGridSpec(
            num_scalar_prefetch=2,
            grid=(nb,),
            in_specs=[pl.BlockSpec((MOE_ROWS * SUBLANES, LANES), xmap), pl.BlockSpec((None, D, de), wmap),
                      pl.BlockSpec((None, D, de), wmap), pl.BlockSpec((None, de, D), wmap)],
            out_specs=pl.BlockSpec((MOE_ROWS * SUBLANES, LANES), lambda i, *_: (i, 0)),
            scratch_shapes=[pltpu.VMEM((D, de), BF16), pltpu.VMEM((D, de), BF16), pltpu.VMEM((de, D), BF16)],
        ),
        out_shape=jax.ShapeDtypeStruct((P * SUBLANES, LANES), F32),
        compiler_params=cparams(dimension_semantics=("arbitrary",)),
        name="experts",
    )(block_e, nbu, xb, w1, w3, w2)

    pos_next = pl.BlockSpec((1, 1, ROW_TILE), lambda i: (jnp.minimum(i + 1, nrow - 1), 0, 0), memory_space=pltpu.SMEM)
    pos_cur = pl.BlockSpec((1, 1, ROW_TILE), lambda i: (i, 0, 0), memory_space=pltpu.SMEM)
    out = pl.pallas_call(
        _combine_kernel,
        grid=(nrow,),
        in_specs=[pl.BlockSpec((ROW_TILE, D), lambda i: (i, 0)), pl.BlockSpec((ROW_TILE, LANES), lambda i: (i, 0)),
                  pos_cur, pos_cur, pos_next, pos_next, pl.BlockSpec(memory_space=pl.ANY)],
        out_specs=pl.BlockSpec((ROW_TILE, D), lambda i: (i, 0)),
        out_shape=jax.ShapeDtypeStruct((T, D), F32),
        scratch_shapes=[pltpu.VMEM((2, TOP_K, ROW_TILE * SUBLANES, LANES), F32), pltpu.SemaphoreType.DMA((2,))],
        compiler_params=cparams(dimension_semantics=("arbitrary",)),
        name="combine",
    )(x2, gcol, pos[0], pos[1], pos[0], pos[1], yb)
    return out.reshape(B, S, D)
```

```python
import functools

import jax
import jax.numpy as jnp
from jax import lax
from jax.experimental import pallas as pl
from jax.experimental.pallas import tpu as pltpu

F32 = jnp.float32
BF16 = jnp.bfloat16
I32 = jnp.int32

EPS = 1e-6
HEAD_DIM = 64
ATTN_BLOCK = 128
X_HEADS = 4
N_GROUPS = 4
EXPERTS_PER_GROUP = 8
N_EXPERTS = N_GROUPS * EXPERTS_PER_GROUP
TOP_K = 2
LANES = 128
SUBLANES = 8

SEQ_TILE = 512
MOE_ROWS = 512
RANK_TILE = 1024
ROW_TILE = 512
NEG = -1e30
VMEM_LIMIT = 56 * 1024 * 1024


def _dot(a, b):
    return lax.dot_general(a, b, (((1,), (0,)), ((), ())), preferred_element_type=F32)


def _dot_nt(a, b):
    return lax.dot_general(a, b, (((1,), (1,)), ((), ())), preferred_element_type=F32)


def _rms(x, gain):
    return x * lax.rsqrt(jnp.mean(x * x, axis=-1, keepdims=True) + EPS) * gain


def _gelu(x):
    return x * (0.5 * (1.0 + jnp.tanh(0.7978845608028654 * (x + 0.044715 * (x * x * x)))))


def _group_sumsq(v, ones_blockdiag):
    sq = v * v
    hi = sq.astype(BF16)
    lo = (sq - hi.astype(F32)).astype(BF16)
    return _dot(hi, ones_blockdiag) + _dot(lo, ones_blockdiag)


def _memkv_kernel(mem_ref, nmem_ref, wck_ref, wcv_ref, k_ref, v_ref):
    d = mem_ref.shape[-1]
    hd = d // X_HEADS
    mn = _rms(mem_ref[...], nmem_ref[...]).astype(BF16)
    k = _dot(mn, wck_ref[...])
    parts = []
    for h in range(X_HEADS):
        kh = k[:, h * hd:(h + 1) * hd]
        parts.append(kh * lax.rsqrt(jnp.mean(kh * kh, axis=-1, keepdims=True) + EPS))
    k_ref[...] = jnp.concatenate(parts, axis=1).astype(BF16)
    v_ref[...] = _dot(mn, wcv_ref[...]).astype(BF16)


def _mixer_kernel(sinks_ref, x_ref, nmix_ref, win_ref, bdq_ref, bdk_ref, gq_ref, sgun_ref, sguw_ref, sgub_ref,
                  ona_ref, ons_ref, wout_ref, o_ref, kprev, vprev):
    j = pl.program_id(1)
    ts = x_ref.shape[0]
    nblk = ts // ATTN_BLOCK
    aw = gq_ref.shape[-1]
    kvw = bdk_ref.shape[-1]
    sw = sgun_ref.shape[-1]

    @pl.when(j == 0)
    def _():
        kprev[...] = jnp.zeros_like(kprev)
        vprev[...] = jnp.zeros_like(vprev)

    x = x_ref[...]
    h = _rms(x, nmix_ref[...]).astype(BF16)
    proj = _dot(h, win_ref[...])
    c0, c1, c2, c3 = aw, aw + kvw, aw + 2 * kvw, aw + 2 * kvw + sw
    q = proj[:, :c0]
    kd = proj[:, c0:c1]
    vd_b = proj[:, c1:c2].astype(BF16)
    u = _gelu(proj[:, c2:c3])
    g = _gelu(proj[:, c3:])

    qn = q * lax.rsqrt(_group_sumsq(q, bdq_ref[...]) * (1.0 / HEAD_DIM) + EPS) * gq_ref[...]
    kd_b = (kd * lax.rsqrt(_group_sumsq(kd, bdk_ref[...]) * (1.0 / HEAD_DIM) + EPS)).astype(BF16)
    lane_q = lax.broadcasted_iota(I32, (1, aw), 1)
    even_q = (lane_q & HEAD_DIM) == 0
    q_even = jnp.where(even_q, qn, 0.0).astype(BF16)
    q_odd = jnp.where(even_q, 0.0, qn).astype(BF16)

    lane = lax.broadcasted_iota(I32, (1, LANES), 1)
    even_l = lane < HEAD_DIM
    qi = lax.broadcasted_iota(I32, (ATTN_BLOCK, 2 * ATTN_BLOCK), 0)
    kj = lax.broadcasted_iota(I32, (ATTN_BLOCK, 2 * ATTN_BLOCK), 1)
    band = (kj > qi) & (kj <= qi + ATTN_BLOCK)
    first_key = jnp.where(j > 0, 0, ATTN_BLOCK)
    band_first = band & (kj >= first_key)

    n_kv = kvw // LANES
    rep = aw // (n_kv * HEAD_DIM)
    attn_rows = []
    for n in range(nblk):
        rows = slice(n * ATTN_BLOCK, (n + 1) * ATTN_BLOCK)
        prev = slice((n - 1) * ATTN_BLOCK, n * ATTN_BLOCK)
        mask = band_first if n == 0 else band
        cols = []
        for hk in range(n_kv):
            kvc = slice(hk * LANES, (hk + 1) * LANES)
            k_prev = kprev[:, kvc] if n == 0 else kd_b[prev, kvc]
            v_prev = vprev[:, kvc] if n == 0 else vd_b[prev, kvc]
            kwin = jnp.concatenate([k_prev, kd_b[rows, kvc]], axis=0)
            vwin = jnp.concatenate([v_prev, vd_b[rows, kvc]], axis=0)
            qparts = []
            for r in range(rep):
                hq = hk * rep + r
                pc = slice((hq // 2) * LANES, (hq // 2 + 1) * LANES)
                qparts.append((q_even if hq % 2 == 0 else q_odd)[rows, pc])
            s = _dot_nt(jnp.concatenate(qparts, axis=0), kwin)
            ps, dens = [], []
            for r in range(rep):
                sr = jnp.where(mask, s[r * ATTN_BLOCK:(r + 1) * ATTN_BLOCK], NEG)
                sink = sinks_ref[hk * rep + r]
                m = jnp.maximum(jnp.max(sr, axis=-1, keepdims=True), sink)
                p = jnp.exp(sr - m)
                dens.append(jnp.sum(p, axis=-1, keepdims=True) + jnp.exp(sink - m))
                ps.append(p.astype(BF16))
            o = _dot(jnp.concatenate(ps, axis=0), vwin)
            outs = [o[r * ATTN_BLOCK:(r + 1) * ATTN_BLOCK] / dens[r] for r in range(rep)]
            for r in range(0, rep, 2):
                cols.append(jnp.where(even_l, outs[r], outs[r + 1]))
        attn_rows.append(jnp.concatenate(cols, axis=1))
    attn = jnp.concatenate(attn_rows, axis=0)
    last = slice(ts - ATTN_BLOCK, ts)
    kprev[...] = kd_b[last]
    vprev[...] = vd_b[last]

    vc_b = _rms(g, sgun_ref[...]).astype(BF16)
    ri = lax.broadcasted_iota(I32, (2 * ATTN_BLOCK, ATTN_BLOCK), 0)
    ci = lax.broadcasted_iota(I32, (2 * ATTN_BLOCK, ATTN_BLOCK), 1)
    causal = (ri & (ATTN_BLOCK - 1)) >= ci
    n_pair = sw // LANES
    wst = [jnp.where(causal, sguw_ref[p], 0.0).astype(BF16) for p in range(n_pair)]
    s_rows = []
    for c in range(nblk):
        rows = slice(c * ATTN_BLOCK, (c + 1) * ATTN_BLOCK)
        cols = []
        for p in range(n_pair):
            res = _dot(wst[p], vc_b[rows, p * LANES:(p + 1) * LANES])
            cols.append(jnp.where(even_l, res[:ATTN_BLOCK], res[ATTN_BLOCK:]))
        s_rows.append(jnp.concatenate(cols, axis=1) + sgub_ref[...])
    sg = u * jnp.concatenate(s_rows, axis=0)

    an = _rms(attn, ona_ref[...]).astype(BF16)
    sn = _rms(sg, ons_ref[...]).astype(BF16)
    o_ref[...] = x + _dot(an, wout_ref[:aw, :]) + _dot(sn, wout_ref[aw:, :])


def _xattn_kernel(x_ref, ncross_ref, wcq_ref, gqk_ref, kx_ref, vx_ref, wco_ref, nffn_ref, wr1_ref, wr2_ref,
                  o_ref, ids_ref, gcol_ref):
    ts, d = x_ref.shape
    hd = d // X_HEADS
    x = x_ref[...]
    h = _rms(x, ncross_ref[...]).astype(BF16)
    q = _dot(h, wcq_ref[...])
    outs = []
    for hh in range(X_HEADS):
        hc = slice(hh * hd, (hh + 1) * hd)
        qh = q[:, hc]
        qn = (qh * lax.rsqrt(jnp.mean(qh * qh, axis=-1, keepdims=True) + EPS) * gqk_ref[...]).astype(BF16)
        s = _dot_nt(qn, kx_ref[:, hc])
        p = jnp.exp(s - jnp.max(s, axis=-1, keepdims=True))
        den = jnp.sum(p, axis=-1, keepdims=True)
        outs.append(_dot(p.astype(BF16), vx_ref[:, hc]) / den)
    x2 = x + _dot(jnp.concatenate(outs, axis=1).astype(BF16), wco_ref[...])
    o_ref[...] = x2

    h3 = _rms(x2, nffn_ref[...])
    h_hi = h3.astype(BF16)
    h_lo = (h3 - h_hi.astype(F32)).astype(BF16)
    nr = wr2_ref.shape[0]
    r1 = _dot_nt(wr1_ref[...], h_hi)
    logits = r1[:nr] + r1[nr:] + _dot_nt(wr2_ref[...], h_lo)
    ge = EXPERTS_PER_GROUP
    sub = lax.broadcasted_iota(I32, (ge, ts), 0).astype(F32)
    grp = jnp.where(sub < N_GROUPS, logits[N_EXPERTS:N_EXPERTS + ge], NEG)
    gmax = jnp.max(grp, axis=0, keepdims=True)
    gidx = jnp.min(jnp.where(grp == gmax, sub, float(ge)), axis=0, keepdims=True)
    g_w = 1.0 / jnp.sum(jnp.exp(grp - gmax), axis=0, keepdims=True)
    local = logits[(N_GROUPS - 1) * ge:N_GROUPS * ge]
    for gi in range(N_GROUPS - 2, -1, -1):
        local = jnp.where(gidx == float(gi), logits[gi * ge:(gi + 1) * ge], local)
    m1 = jnp.max(local, axis=0, keepdims=True)
    i1 = jnp.min(jnp.where(local == m1, sub, float(ge)), axis=0, keepdims=True)
    rest = jnp.where(sub == i1, NEG, local)
    m2 = jnp.max(rest, axis=0, keepdims=True)
    i2 = jnp.min(jnp.where(rest == m2, sub, float(ge)), axis=0, keepdims=True)
    e2 = jnp.exp(m2 - m1)
    den = 1.0 + e2
    w1 = g_w * (1.0 / den)
    w2 = g_w * (e2 / den)
    ids_ref[...] = jnp.where(sub == 0.0, gidx * ge + i1, jnp.where(sub == 1.0, gidx * ge + i2, 0.0)).astype(I32)
    rows = lax.broadcasted_iota(I32, (LANES, ts), 0)
    gates_t = jnp.where(rows == 0, w1, jnp.where(rows == 1, w2, 0.0))
    gcol_ref[...] = gates_t.T


def _rank_kernel(ids_ref, rank_ref, cnt_ref, carry, upper):
    i = pl.program_id(0)
    ct = ids_ref.shape[-1]

    @pl.when(i == 0)
    def _():
        carry[...] = jnp.zeros_like(carry)
        r = lax.broadcasted_iota(I32, (ct, ct), 0)
        c = lax.broadcasted_iota(I32, (ct, ct), 1)
        upper[...] = jnp.where(r < c, 1.0, 0.0).astype(BF16)

    e = ids_ref[...]
    onehot = lax.broadcasted_iota(I32, (N_EXPERTS, ct), 0) == e
    oh = jnp.where(onehot, 1.0, 0.0)
    before = _dot(oh.astype(BF16), upper[...]) + carry[:, 0:1]
    rank_ref[...] = jnp.sum(jnp.where(onehot, before, 0.0), axis=0, keepdims=True).astype(I32)
    carry[...] = carry[...] + jnp.sum(oh, axis=1, keepdims=True)

    @pl.when(i == pl.num_programs(0) - 1)
    def _():
        cnt_ref[...] = carry[...]


def _row_copy(src, dst, sem):
    return pltpu.make_async_copy(src, dst, sem)


def _dispatch_kernel(fs_ref, fl_ref, tail_ref, x_ref, nffn_ref, pos0_ref, pos1_ref, xb_ref, buf, zbuf, sem, zsem):
    i = pl.program_id(0)
    n = pl.num_programs(0)
    td = x_ref.shape[0]
    slot = i % 2
    zrows = zbuf.shape[0] // SUBLANES
    chunk_bits = [1 << b for b in range((MOE_ROWS - 1).bit_length())]

    def tile_rows(ref, first, count=1):
        return ref.at[pl.ds(pl.multiple_of(first * SUBLANES, SUBLANES), count * SUBLANES)]

    def wait_rows(s):
        def body(r, c):
            _row_copy(tile_rows(buf.at[s], 0), tile_rows(xb_ref, 0), sem.at[s]).wait()
            _row_copy(tile_rows(buf.at[s], 0), tile_rows(xb_ref, 0), sem.at[s]).wait()
            return c
        lax.fori_loop(0, td, body, 0, unroll=8)

    def zero_fill(e, do):
        start, length = fs_ref[e], fl_ref[e]
        for bit in chunk_bits:
            @pl.when((length & bit) != 0)
            def _(bit=bit):
                off = start + (length & ~(2 * bit - 1))
                do(_row_copy(tile_rows(zbuf, 0, bit), tile_rows(xb_ref, off, bit), zsem))

    def zero_tail(do):
        def body(c, carry):
            do(_row_copy(zbuf, tile_rows(xb_ref, tail_ref[0] + c * zrows, zrows), zsem))
            return carry
        lax.fori_loop(0, tail_ref[1], body, 0)

    def for_experts(do):
        def body(e, carry):
            zero_fill(e, do)
            return carry
        lax.fori_loop(0, N_EXPERTS, body, 0)

    @pl.when(i == 0)
    def _():
        zbuf[...] = jnp.zeros_like(zbuf)
        for_experts(lambda c: c.start())
        zero_tail(lambda c: c.start())
        for_experts(lambda c: c.wait())
        zero_tail(lambda c: c.wait())

    @pl.when(i >= 2)
    def _():
        wait_rows(slot)

    h3 = _rms(x_ref[...], nffn_ref[...])
    for c in range(SUBLANES):
        buf[slot, pl.ds(c, td, stride=SUBLANES), :] = h3[:, c * LANES:(c + 1) * LANES]

    def issue(r, c):
        src = tile_rows(buf.at[slot], r)
        _row_copy(src, tile_rows(xb_ref, pos0_ref[0, 0, r]), sem.at[slot]).start(priority=0)
        _row_copy(src, tile_rows(xb_ref, pos1_ref[0, 0, r]), sem.at[slot]).start(priority=1)
        return c
    lax.fori_loop(0, td, issue, 0, unroll=8)

    @pl.when(i == n - 1)
    def _():
        wait_rows(slot)

        @pl.when(n >= 2)
        def _():
            wait_rows(1 - slot)


def _expert_kernel(first_ref, nblk_ref, tot_ref, xb_ref, w1_ref, w3_ref, w2_ref, yb_ref,
                   xbuf, ybuf, w1b, w3b, w2b, xsem, ysem):
    e = pl.program_id(0)
    nbu = tot_ref[0]
    rows = xbuf.shape[1] // SUBLANES
    blk_rows = xbuf.shape[1]

    def block(ref, gb):
        return ref.at[pl.ds(pl.multiple_of(gb * blk_rows, blk_rows), blk_rows)]

    def xcopy(gb, slot):
        return pltpu.make_async_copy(block(xb_ref, gb), xbuf.at[slot], xsem.at[slot])

    def ycopy(gb, slot):
        return pltpu.make_async_copy(ybuf.at[slot], block(yb_ref, gb), ysem.at[slot])

    @pl.when(e == 0)
    def _():
        xcopy(0, 0).start()

    w1b[...] = w1_ref[...].astype(BF16)
    w3b[...] = w3_ref[...].astype(BF16)
    w2b[...] = w2_ref[...].astype(BF16)

    def body(b, carry):
        gb = first_ref[e] + b
        slot = gb % 2
        xcopy(gb, slot).wait()

        @pl.when(gb + 1 < nbu)
        def _():
            xcopy(gb + 1, 1 - slot).start()

        @pl.when(gb >= 2)
        def _():
            ycopy(gb - 2, slot).wait()

        x = jnp.concatenate([xbuf[slot, pl.ds(c, rows, stride=SUBLANES), :] for c in range(SUBLANES)], axis=1)
        x = x.astype(BF16)
        a = _dot(x, w1b[...])
        g = _dot(x, w3b[...])
        hid = (a * (1.0 / (1.0 + jnp.exp(-a)))) * g
        y = _dot(hid.astype(BF16), w2b[...])
        for c in range(SUBLANES):
            ybuf[slot, pl.ds(c, rows, stride=SUBLANES), :] = y[:, c * LANES:(c + 1) * LANES]
        ycopy(gb, slot).start()
        return carry
    lax.fori_loop(0, nblk_ref[e], body, 0)

    @pl.when(e == pl.num_programs(0) - 1)
    def _():
        ycopy(nbu - 2, nbu % 2).wait()
        ycopy(nbu - 1, (nbu - 1) % 2).wait()
        ybuf[0] = jnp.zeros(ybuf.shape[1:], ybuf.dtype)

        def fill(do):
            def tail(c, carry):
                do(ycopy(nbu + c, 0))
                return carry
            lax.fori_loop(0, tot_ref[1], tail, 0)
        fill(lambda c: c.start())
        fill(lambda c: c.wait())


def _combine_kernel(x_ref, g_ref, pos0_ref, pos1_ref, pos0n_ref, pos1n_ref, yb_ref, o_ref, ybuf, sem):
    i = pl.program_id(0)
    n = pl.num_programs(0)
    tf = x_ref.shape[0]
    slot = i % 2

    def tile_rows(ref, first):
        return ref.at[pl.ds(pl.multiple_of(first * SUBLANES, SUBLANES), SUBLANES)]

    def issue(s, p0_ref, p1_ref):
        def body(r, c):
            _row_copy(tile_rows(yb_ref, p0_ref[0, 0, r]), tile_rows(ybuf.at[s, 0], r), sem.at[s]).start(priority=0)
            _row_copy(tile_rows(yb_ref, p1_ref[0, 0, r]), tile_rows(ybuf.at[s, 1], r), sem.at[s]).start(priority=1)
            return c
        lax.fori_loop(0, tf, body, 0, unroll=8)

    @pl.when(i == 0)
    def _():
        issue(0, pos0_ref, pos1_ref)

    @pl.when(i + 1 < n)
    def _():
        issue(1 - slot, pos0n_ref, pos1n_ref)

    def wait_body(r, c):
        _row_copy(tile_rows(yb_ref, 0), tile_rows(ybuf.at[slot, 0], 0), sem.at[slot]).wait()
        _row_copy(tile_rows(yb_ref, 0), tile_rows(ybuf.at[slot, 1], 0), sem.at[slot]).wait()
        return c
    lax.fori_loop(0, tf, wait_body, 0, unroll=8)

    def rows_of(k):
        return jnp.concatenate([ybuf[slot, k, pl.ds(c, tf, stride=SUBLANES), :] for c in range(SUBLANES)], axis=1)

    g = g_ref[...]
    o_ref[...] = x_ref[...] + (g[:, 0:1] * rows_of(0) + g[:, 1:2] * rows_of(1))


def _const_spec(shape):
    return pl.BlockSpec(shape, lambda *_: (0,) * len(shape))


def _blockdiag_ones(width):
    r = jnp.arange(width) // HEAD_DIM
    return (r[:, None] == r[None, :]).astype(BF16)


def kernel(x, mem, norm_mix, w_in, q_norm, k_norm, attn_sinks, sgu_norm, sgu_w, sgu_b, out_norm_attn, out_norm_sgu, w_out, norm_cross, norm_mem, w_cq, w_ck, w_cv, cq_norm, ck_norm, w_co, norm_ffn, w_router_group, w_router_expert, w1, w3, w2):
    B, S, D = x.shape
    M = mem.shape[1]
    T = B * S
    A = T * TOP_K
    n_q = attn_sinks.shape[0]
    aw = n_q * HEAD_DIM
    sw = sgu_norm.shape[0]
    n_kv = (w_in.shape[1] - aw - 2 * sw) // (2 * HEAD_DIM)
    kvw = n_kv * HEAD_DIM
    chunk = sgu_w.shape[1]
    de = w1.shape[2]
    assert chunk == ATTN_BLOCK and S % SEQ_TILE == 0 and T % ROW_TILE == 0 and A % RANK_TILE == 0
    assert n_kv * 2 * HEAD_DIM == n_kv * LANES and sgu_w.shape[0] * HEAD_DIM == sw
    assert D == SUBLANES * LANES
    row = lambda v: v.reshape(1, -1).astype(F32)
    cparams = functools.partial(pltpu.CompilerParams, vmem_limit_bytes=VMEM_LIMIT)

    dup = lambda w: jnp.repeat(w.reshape(D, n_kv, 1, HEAD_DIM), 2, axis=2).reshape(D, 2 * kvw)
    c0, c1, c2 = aw, aw + kvw, aw + 2 * kvw
    win = jnp.concatenate([w_in[:, :c0], dup(w_in[:, c0:c1]), dup(w_in[:, c1:c2]), w_in[:, c2:]], axis=1).astype(BF16)
    gq = row(jnp.tile(q_norm * k_norm * (HEAD_DIM ** -0.5), n_q))
    sguw_pairs = sgu_w.reshape(sw // LANES, 2 * chunk, chunk).astype(F32)
    sgub_full = jnp.repeat(sgu_b.T, HEAD_DIM, axis=1).astype(F32)
    hd = D // X_HEADS
    gqk = row(cq_norm * ck_norm * (hd ** -0.5))
    wr = jnp.concatenate([w_router_expert.transpose(0, 2, 1).reshape(N_EXPERTS, D), w_router_group.T,
                          jnp.zeros((EXPERTS_PER_GROUP - N_GROUPS, D), F32)], axis=0).astype(F32)
    wr_hi = wr.astype(BF16)
    wr_lo = (wr - wr_hi.astype(F32)).astype(BF16)
    wr1 = jnp.concatenate([wr_hi, wr_lo], axis=0)
    nr = wr.shape[0]

    kx, vx = pl.pallas_call(
        _memkv_kernel,
        grid=(B,),
        in_specs=[pl.BlockSpec((None, M, D), lambda b: (b, 0, 0)), _const_spec((1, D)), _const_spec((D, D)),
                  _const_spec((D, D))],
        out_specs=[pl.BlockSpec((None, M, D), lambda b: (b, 0, 0))] * 2,
        out_shape=[jax.ShapeDtypeStruct((B, M, D), BF16)] * 2,
        compiler_params=cparams(dimension_semantics=("arbitrary",)),
        name="memkv",
    )(mem, row(norm_mem), w_ck.astype(BF16), w_cv.astype(BF16))

    nseq = S // SEQ_TILE
    xtile = pl.BlockSpec((None, SEQ_TILE, D), lambda b, j: (b, j, 0))
    x1 = pl.pallas_call(
        _mixer_kernel,
        grid=(B, nseq),
        in_specs=[pl.BlockSpec(memory_space=pltpu.SMEM), xtile, _const_spec((1, D)), _const_spec(win.shape),
                  _const_spec((aw, aw)), _const_spec((2 * kvw, 2 * kvw)), _const_spec((1, aw)), _const_spec((1, sw)),
                  _const_spec(sguw_pairs.shape), _const_spec(sgub_full.shape), _const_spec((1, aw)),
                  _const_spec((1, sw)), _const_spec((D, D))],
        out_specs=xtile,
        out_shape=jax.ShapeDtypeStruct((B, S, D), F32),
        scratch_shapes=[pltpu.VMEM((ATTN_BLOCK, 2 * kvw), BF16), pltpu.VMEM((ATTN_BLOCK, 2 * kvw), BF16)],
        compiler_params=cparams(dimension_semantics=("arbitrary", "arbitrary")),
        name="mixer",
    )(attn_sinks.astype(F32), x, row(norm_mix), win, _blockdiag_ones(aw), _blockdiag_ones(2 * kvw), gq,
      row(sgu_norm), sguw_pairs, sgub_full, row(out_norm_attn), row(out_norm_sgu), w_out.astype(BF16))

    x2, ids, gcol = pl.pallas_call(
        _xattn_kernel,
        grid=(B, nseq),
        in_specs=[xtile, _const_spec((1, D)), _const_spec((D, D)), _const_spec((1, hd)),
                  pl.BlockSpec((None, M, D), lambda b, j: (b, 0, 0)), pl.BlockSpec((None, M, D), lambda b, j: (b, 0, 0)),
                  _const_spec((D, D)), _const_spec((1, D)), _const_spec((2 * nr, D)), _const_spec((nr, D))],
        out_specs=[xtile, pl.BlockSpec((EXPERTS_PER_GROUP, SEQ_TILE), lambda b, j: (0, b * nseq + j)),
                   pl.BlockSpec((SEQ_TILE, LANES), lambda b, j: (b * nseq + j, 0))],
        out_shape=[jax.ShapeDtypeStruct((B, S, D), F32), jax.ShapeDtypeStruct((EXPERTS_PER_GROUP, T), I32),
                   jax.ShapeDtypeStruct((T, LANES), F32)],
        compiler_params=cparams(dimension_semantics=("arbitrary", "arbitrary")),
        name="xattn",
    )(x1, row(norm_cross), w_cq.astype(BF16), gqk, kx, vx, w_co.astype(BF16), row(norm_ffn), wr1, wr_hi)
    x2 = x2.reshape(T, D)
    ids = ids[:TOP_K]

    rank, cnt = pl.pallas_call(
        _rank_kernel,
        grid=(A // RANK_TILE,),
        in_specs=[pl.BlockSpec((1, RANK_TILE), lambda i: (0, i))],
        out_specs=[pl.BlockSpec((1, RANK_TILE), lambda i: (0, i)), _const_spec((N_EXPERTS, LANES))],
        out_shape=[jax.ShapeDtypeStruct((1, A), I32), jax.ShapeDtypeStruct((N_EXPERTS, LANES), F32)],
        scratch_shapes=[pltpu.VMEM((N_EXPERTS, LANES), F32), pltpu.VMEM((RANK_TILE, RANK_TILE), BF16)],
        compiler_params=cparams(dimension_semantics=("arbitrary",)),
        name="rank",
    )(ids.reshape(1, A))

    counts = cnt[:, 0].astype(I32)
    padded = (counts + MOE_ROWS - 1) // MOE_ROWS * MOE_ROWS
    pad_end = jnp.cumsum(padded)
    pad_start = pad_end - padded
    nb = A // MOE_ROWS + N_EXPERTS
    nbu = (pad_end[-1:] // MOE_ROWS).astype(I32)
    first_blk = (pad_start // MOE_ROWS).astype(I32)
    n_blk = (padded // MOE_ROWS).astype(I32)
    blk_tot = jnp.concatenate([nbu, nb - nbu]).astype(I32)
    pos =(jnp.take(pad_start, ids.reshape(-1)) + rank.reshape(-1)).reshape(TOP_K, T // ROW_TILE, 1, ROW_TILE)
    fill_start = (pad_start + counts).astype(I32)
    fill_len = (padded - counts).astype(I32)
    zrows = MOE_ROWS // 2
    tail = jnp.concatenate([pad_end[-1:], (nb - nbu) * (MOE_ROWS // zrows)]).astype(I32)

    nrow = T // ROW_TILE
    P = nb * MOE_ROWS
    pos_spec = pl.BlockSpec((1, 1, ROW_TILE), lambda i, *_: (i, 0, 0), memory_space=pltpu.SMEM)
    xb = pl.pallas_call(
        _dispatch_kernel,
        grid_spec=pltpu.PrefetchScalarGridSpec(
            num_scalar_prefetch=3,
            grid=(nrow,),
            in_specs=[pl.BlockSpec((ROW_TILE, D), lambda i, *_: (i, 0)), pl.BlockSpec((1, D), lambda i, *_: (0, 0)),
                      pos_spec, pos_spec],
            out_specs=pl.BlockSpec(memory_space=pl.ANY),
            scratch_shapes=[pltpu.VMEM((2, ROW_TILE * SUBLANES, LANES), F32), pltpu.VMEM((zrows * SUBLANES, LANES), F32),
                            pltpu.SemaphoreType.DMA((2,)), pltpu.SemaphoreType.DMA(())],
        ),
        out_shape=jax.ShapeDtypeStruct((P * SUBLANES, LANES), F32),
        compiler_params=cparams(dimension_semantics=("arbitrary",)),
        name="dispatch",
    )(fill_start, fill_len, tail, x2, row(norm_ffn), pos[0], pos[1])

    wmap = lambda e, *_: (e, 0, 0)
    blk_buf = pltpu.VMEM((2, MOE_ROWS * SUBLANES, LANES), F32)
    yb = pl.pallas_call(
        _expert_kernel,
        grid_spec=pltpu.PrefetchScalarGridSpec(
            num_scalar_prefetch=3,
            grid=(N_EXPERTS,),
            in_specs=[pl.BlockSpec(memory_space=pl.ANY), pl.BlockSpec((None, D, de), wmap),
                      pl.BlockSpec((None, D, de), wmap), pl.BlockSpec((None, de, D), wmap)],
            out_specs=pl.BlockSpec(memory_space=pl.ANY),
            scratch_shapes=[blk_buf, blk_buf, pltpu.VMEM((D, de), BF16), pltpu.VMEM((D, de), BF16),
                            pltpu.VMEM((de, D), BF16), pltpu.SemaphoreType.DMA((2,)), pltpu.SemaphoreType.DMA((2,))],
        ),
        out_shape=jax.ShapeDtypeStruct((P * SUBLANES, LANES), F32),
        compiler_params=cparams(dimension_semantics=("arbitrary",)),
        name="experts",
    )(first_blk, n_blk, blk_tot, xb, w1, w3, w2)

    pos_next = pl.BlockSpec((1, 1, ROW_TILE), lambda i: (jnp.minimum(i + 1, nrow - 1), 0, 0), memory_space=pltpu.SMEM)
    pos_cur = pl.BlockSpec((1, 1, ROW_TILE), lambda i: (i, 0, 0), memory_space=pltpu.SMEM)
    out = pl.pallas_call(
        _combine_kernel,
        grid=(nrow,),
        in_specs=[pl.BlockSpec((ROW_TILE, D), lambda i: (i, 0)), pl.BlockSpec((ROW_TILE, LANES), lambda i: (i, 0)),
                  pos_cur, pos_cur, pos_next, pos_next, pl.BlockSpec(memory_space=pl.ANY)],
        out_specs=pl.BlockSpec((ROW_TILE, D), lambda i: (i, 0)),
        out_shape=jax.ShapeDtypeStruct((T, D), F32),
        scratch_shapes=[pltpu.VMEM((2, TOP_K, ROW_TILE * SUBLANES, LANES), F32), pltpu.SemaphoreType.DMA((2,))],
        compiler_params=cparams(dimension_semantics=("arbitrary",)),
        name="combine",
    )(x2, gcol, pos[0], pos[1], pos[0], pos[1], yb)
    return out.reshape(B, S, D)
```

```python
import functools

import jax
import jax.numpy as jnp
from jax import lax
from jax.experimental import pallas as pl
from jax.experimental.pallas import tpu as pltpu

F32 = jnp.float32
BF16 = jnp.bfloat16
I32 = jnp.int32
U32 = jnp.uint32

EPS = 1e-6
HEAD_DIM = 64
ATTN_BLOCK = 128
X_HEADS = 4
N_GROUPS = 4
EXPERTS_PER_GROUP = 8
N_EXPERTS = N_GROUPS * EXPERTS_PER_GROUP
TOP_K = 2
LANES = 128
SUBLANES = 8

SEQ_TILE = 512
MOE_ROWS = 512
RANK_TILE = 1024
ROW_TILE = 512
NEG = -1e30
VMEM_LIMIT = 56 * 1024 * 1024


def _dot(a, b):
    return lax.dot_general(a, b, (((1,), (0,)), ((), ())), preferred_element_type=F32)


def _dot_nt(a, b):
    return lax.dot_general(a, b, (((1,), (1,)), ((), ())), preferred_element_type=F32)


def _rms(x, gain):
    return x * lax.rsqrt(jnp.mean(x * x, axis=-1, keepdims=True) + EPS) * gain


def _gelu(x):
    return x * (0.5 * (1.0 + jnp.tanh(0.7978845608028654 * (x + 0.044715 * (x * x * x)))))


def _group_sumsq(v, ones_blockdiag):
    sq = v * v
    hi = sq.astype(BF16)
    lo = (sq - hi.astype(F32)).astype(BF16)
    return _dot(hi, ones_blockdiag) + _dot(lo, ones_blockdiag)


def _pack_bf16_pair(lo, hi):
    def bits(v):
        u = lax.bitcast_convert_type(v, U32)
        return (u + U32(0x7FFF) + ((u >> 16) & U32(1))) >> 16
    return bits(lo) | (bits(hi) << 16)


def _unpack_bf16_pair(w):
    return lax.bitcast_convert_type(w << 16, F32), lax.bitcast_convert_type(w & U32(0xFFFF0000), F32)


def _pack_row_tiles(v):
    half = v.shape[1] // (2 * LANES)
    return [_pack_bf16_pair(v[:, s * LANES:(s + 1) * LANES], v[:, (s + half) * LANES:(s + half + 1) * LANES])
            for s in range(half)]


def _unpack_row_tiles(words):
    pairs = [_unpack_bf16_pair(w) for w in words]
    return jnp.concatenate([p[0] for p in pairs] + [p[1] for p in pairs], axis=1)


def _memkv_kernel(mem_ref, nmem_ref, wck_ref, wcv_ref, k_ref, v_ref):
    d = mem_ref.shape[-1]
    hd = d // X_HEADS
    mn = _rms(mem_ref[...], nmem_ref[...]).astype(BF16)
    k = _dot(mn, wck_ref[...])
    parts = []
    for h in range(X_HEADS):
        kh = k[:, h * hd:(h + 1) * hd]
        parts.append(kh * lax.rsqrt(jnp.mean(kh * kh, axis=-1, keepdims=True) + EPS))
    k_ref[...] = jnp.concatenate(parts, axis=1).astype(BF16)
    v_ref[...] = _dot(mn, wcv_ref[...]).astype(BF16)


def _mixer_kernel(sinks_ref, x_ref, nmix_ref, win_ref, bdq_ref, bdk_ref, gq_ref, sgun_ref, sguw_ref, sgub_ref,
                  ona_ref, ons_ref, wout_ref, o_ref, kprev, vprev):
    j = pl.program_id(1)
    ts = x_ref.shape[0]
    nblk = ts // ATTN_BLOCK
    aw = gq_ref.shape[-1]
    kvw = bdk_ref.shape[-1]
    sw = sgun_ref.shape[-1]

    @pl.when(j == 0)
    def _():
        kprev[...] = jnp.zeros_like(kprev)
        vprev[...] = jnp.zeros_like(vprev)

    x = x_ref[...]
    h = _rms(x, nmix_ref[...]).astype(BF16)
    proj = _dot(h, win_ref[...])
    c0, c1, c2, c3 = aw, aw + kvw, aw + 2 * kvw, aw + 2 * kvw + sw
    q = proj[:, :c0]
    kd = proj[:, c0:c1]
    vd_b = proj[:, c1:c2].astype(BF16)
    u = _gelu(proj[:, c2:c3])
    g = _gelu(proj[:, c3:])

    qn = q * lax.rsqrt(_group_sumsq(q, bdq_ref[...]) * (1.0 / HEAD_DIM) + EPS) * gq_ref[...]
    kd_b = (kd * lax.rsqrt(_group_sumsq(kd, bdk_ref[...]) * (1.0 / HEAD_DIM) + EPS)).astype(BF16)
    lane_q = lax.broadcasted_iota(I32, (1, aw), 1)
    even_q = (lane_q & HEAD_DIM) == 0
    q_even = jnp.where(even_q, qn, 0.0).astype(BF16)
    q_odd = jnp.where(even_q, 0.0, qn).astype(BF16)

    lane = lax.broadcasted_iota(I32, (1, LANES), 1)
    even_l = lane < HEAD_DIM
    qi = lax.broadcasted_iota(I32, (ATTN_BLOCK, 2 * ATTN_BLOCK), 0)
    kj = lax.broadcasted_iota(I32, (ATTN_BLOCK, 2 * ATTN_BLOCK), 1)
    band = (kj > qi) & (kj <= qi + ATTN_BLOCK)
    first_key = jnp.where(j > 0, 0, ATTN_BLOCK)
    band_first = band & (kj >= first_key)

    n_kv = kvw // LANES
    rep = aw // (n_kv * HEAD_DIM)
    attn_rows = []
    for n in range(nblk):
        rows = slice(n * ATTN_BLOCK, (n + 1) * ATTN_BLOCK)
        prev = slice((n - 1) * ATTN_BLOCK, n * ATTN_BLOCK)
        mask = band_first if n == 0 else band
        cols = []
        for hk in range(n_kv):
            kvc = slice(hk * LANES, (hk + 1) * LANES)
            k_prev = kprev[:, kvc] if n == 0 else kd_b[prev, kvc]
            v_prev = vprev[:, kvc] if n == 0 else vd_b[prev, kvc]
            kwin = jnp.concatenate([k_prev, kd_b[rows, kvc]], axis=0)
            vwin = jnp.concatenate([v_prev, vd_b[rows, kvc]], axis=0)
            qparts = []
            for r in range(rep):
                hq = hk * rep + r
                pc = slice((hq // 2) * LANES, (hq // 2 + 1) * LANES)
                qparts.append((q_even if hq % 2 == 0 else q_odd)[rows, pc])
            s = _dot_nt(jnp.concatenate(qparts, axis=0), kwin)
            ps, dens = [], []
            for r in range(rep):
                sr = jnp.where(mask, s[r * ATTN_BLOCK:(r + 1) * ATTN_BLOCK], NEG)
                sink = sinks_ref[hk * rep + r]
                m = jnp.maximum(jnp.max(sr, axis=-1, keepdims=True), sink)
                p = jnp.exp(sr - m)
                dens.append(jnp.sum(p, axis=-1, keepdims=True) + jnp.exp(sink - m))
                ps.append(p.astype(BF16))
            o = _dot(jnp.concatenate(ps, axis=0), vwin)
            outs = [o[r * ATTN_BLOCK:(r + 1) * ATTN_BLOCK] / dens[r] for r in range(rep)]
            for r in range(0, rep, 2):
                cols.append(jnp.where(even_l, outs[r], outs[r + 1]))
        attn_rows.append(jnp.concatenate(cols, axis=1))
    attn = jnp.concatenate(attn_rows, axis=0)
    last = slice(ts - ATTN_BLOCK, ts)
    kprev[...] = kd_b[last]
    vprev[...] = vd_b[last]

    vc_b = _rms(g, sgun_ref[...]).astype(BF16)
    ri = lax.broadcasted_iota(I32, (2 * ATTN_BLOCK, ATTN_BLOCK), 0)
    ci = lax.broadcasted_iota(I32, (2 * ATTN_BLOCK, ATTN_BLOCK), 1)
    causal = (ri & (ATTN_BLOCK - 1)) >= ci
    n_pair = sw // LANES
    wst = [jnp.where(causal, sguw_ref[p], 0.0).astype(BF16) for p in range(n_pair)]
    s_rows = []
    for c in range(nblk):
        rows = slice(c * ATTN_BLOCK, (c + 1) * ATTN_BLOCK)
        cols = []
        for p in range(n_pair):
            res = _dot(wst[p], vc_b[rows, p * LANES:(p + 1) * LANES])
            cols.append(jnp.where(even_l, res[:ATTN_BLOCK], res[ATTN_BLOCK:]))
        s_rows.append(jnp.concatenate(cols, axis=1) + sgub_ref[...])
    sg = u * jnp.concatenate(s_rows, axis=0)

    an = _rms(attn, ona_ref[...]).astype(BF16)
    sn = _rms(sg, ons_ref[...]).astype(BF16)
    o_ref[...] = x + _dot(an, wout_ref[:aw, :]) + _dot(sn, wout_ref[aw:, :])


def _xattn_kernel(x_ref, ncross_ref, wcq_ref, gqk_ref, kx_ref, vx_ref, wco_ref, nffn_ref, wr1_ref, wr2_ref,
                  o_ref, ids_ref, gcol_ref, cnt_ref):
    ts, d = x_ref.shape
    hd = d // X_HEADS
    x = x_ref[...]
    h = _rms(x, ncross_ref[...]).astype(BF16)
    q = _dot(h, wcq_ref[...])
    outs = []
    for hh in range(X_HEADS):
        hc = slice(hh * hd, (hh + 1) * hd)
        qh = q[:, hc]
        qn = (qh * lax.rsqrt(jnp.mean(qh * qh, axis=-1, keepdims=True) + EPS) * gqk_ref[...]).astype(BF16)
        s = _dot_nt(qn, kx_ref[:, hc])
        p = jnp.exp(s - jnp.max(s, axis=-1, keepdims=True))
        den = jnp.sum(p, axis=-1, keepdims=True)
        outs.append(_dot(p.astype(BF16), vx_ref[:, hc]) / den)
    x2 = x + _dot(jnp.concatenate(outs, axis=1).astype(BF16), wco_ref[...])
    o_ref[...] = x2

    h3 = _rms(x2, nffn_ref[...])
    h_hi = h3.astype(BF16)
    h_lo = (h3 - h_hi.astype(F32)).astype(BF16)
    nr = wr2_ref.shape[0]
    r1 = _dot_nt(wr1_ref[...], h_hi)
    logits = r1[:nr] + r1[nr:] + _dot_nt(wr2_ref[...], h_lo)
    ge = EXPERTS_PER_GROUP
    sub = lax.broadcasted_iota(I32, (ge, ts), 0).astype(F32)
    grp = jnp.where(sub < N_GROUPS, logits[N_EXPERTS:N_EXPERTS + ge], NEG)
    gmax = jnp.max(grp, axis=0, keepdims=True)
    gidx = jnp.min(jnp.where(grp == gmax, sub, float(ge)), axis=0, keepdims=True)
    g_w = 1.0 / jnp.sum(jnp.exp(grp - gmax), axis=0, keepdims=True)
    local = logits[(N_GROUPS - 1) * ge:N_GROUPS * ge]
    for gi in range(N_GROUPS - 2, -1, -1):
        local = jnp.where(gidx == float(gi), logits[gi * ge:(gi + 1) * ge], local)
    m1 = jnp.max(local, axis=0, keepdims=True)
    i1 = jnp.min(jnp.where(local == m1, sub, float(ge)), axis=0, keepdims=True)
    rest = jnp.where(sub == i1, NEG, local)
    m2 = jnp.max(rest, axis=0, keepdims=True)
    i2 = jnp.min(jnp.where(rest == m2, sub, float(ge)), axis=0, keepdims=True)
    e2 = jnp.exp(m2 - m1)
    den = 1.0 + e2
    w1 = g_w * (1.0 / den)
    w2 = g_w * (e2 / den)
    id1, id2 = gidx * ge + i1, gidx * ge + i2
    ids_ref[...] = jnp.where(sub == 0.0, id1, jnp.where(sub == 1.0, id2, 0.0)).astype(I32)
    rows = lax.broadcasted_iota(I32, (LANES, ts), 0)
    gates_t = jnp.where(rows == 0, w1, jnp.where(rows == 1, w2, 0.0))
    gcol_ref[...] = gates_t.T

    ex = lax.broadcasted_iota(I32, (N_EXPERTS, ts), 0).astype(F32)
    hits = jnp.where(ex == id1, 1.0, 0.0) + jnp.where(ex == id2, 1.0, 0.0)

    @pl.when(jnp.logical_and(pl.program_id(0) == 0, pl.program_id(1) == 0))
    def _():
        cnt_ref[...] = jnp.zeros_like(cnt_ref)
    cnt_ref[...] = cnt_ref[...] + jnp.sum(hits, axis=1, keepdims=True)


def _rank_kernel(ids_ref, pstart_ref, pos_ref, carry, upper):
    i = pl.program_id(0)
    ct = ids_ref.shape[-1]

    @pl.when(i == 0)
    def _():
        carry[...] = jnp.zeros_like(carry)
        r = lax.broadcasted_iota(I32, (ct, ct), 0)
        c = lax.broadcasted_iota(I32, (ct, ct), 1)
        upper[...] = jnp.where(r < c, 1.0, 0.0).astype(BF16)

    e = ids_ref[...]
    onehot = lax.broadcasted_iota(I32, (N_EXPERTS, ct), 0) == e
    oh = jnp.where(onehot, 1.0, 0.0)
    row = _dot(oh.astype(BF16), upper[...]) + (carry[:, 0:1] + pstart_ref[:, 0:1])
    pos_ref[...] = jnp.sum(jnp.where(onehot, row, 0.0), axis=0, keepdims=True).astype(I32)
    carry[...] = carry[...] + jnp.sum(oh, axis=1, keepdims=True)


def _row_copy(src, dst, sem):
    return pltpu.make_async_copy(src, dst, sem)


def _dispatch_kernel(fs_ref, fl_ref, tail_ref, x_ref, nffn_ref, pos0_ref, pos1_ref, xb_ref, buf, zbuf, sem, zsem):
    i = pl.program_id(0)
    n = pl.num_programs(0)
    td = x_ref.shape[0]
    slot = i % 2
    zrows, half = zbuf.shape[0], zbuf.shape[1]
    chunk_bits = [1 << b for b in range((MOE_ROWS - 1).bit_length())]

    def slab(s, r):
        return buf.at[s, pl.ds(pl.multiple_of(r * SUBLANES, SUBLANES), half)]

    def wait_rows(s):
        def body(r, c):
            _row_copy(slab(s, 0), xb_ref.at[0], sem.at[s]).wait()
            _row_copy(slab(s, 0), xb_ref.at[0], sem.at[s]).wait()
            return c
        lax.fori_loop(0, td, body, 0, unroll=8)

    def zero_fill(e, do):
        start, length = fs_ref[e], fl_ref[e]
        for bit in chunk_bits:
            @pl.when((length & bit) != 0)
            def _(bit=bit):
                off = start + (length & ~(2 * bit - 1))
                do(_row_copy(zbuf.at[pl.ds(0, bit)], xb_ref.at[pl.ds(off, bit)], zsem))

    def zero_tail(do):
        def body(c, carry):
            do(_row_copy(zbuf, xb_ref.at[pl.ds(tail_ref[0] + c * zrows, zrows)], zsem))
            return carry
        lax.fori_loop(0, tail_ref[1], body, 0)

    def for_experts(do):
        def body(e, carry):
            zero_fill(e, do)
            return carry
        lax.fori_loop(0, N_EXPERTS, body, 0)

    @pl.when(i == 0)
    def _():
        zbuf[...] = jnp.zeros_like(zbuf)
        for_experts(lambda c: c.start())
        zero_tail(lambda c: c.start())
        for_experts(lambda c: c.wait())
        zero_tail(lambda c: c.wait())

    @pl.when(i >= 2)
    def _():
        wait_rows(slot)

    for s, words in enumerate(_pack_row_tiles(_rms(x_ref[...], nffn_ref[...]))):
        buf[slot, pl.ds(s, td, stride=SUBLANES), :] = words

    def issue(r, c):
        src = slab(slot, r)
        _row_copy(src, xb_ref.at[pos0_ref[0, 0, r]], sem.at[slot]).start(priority=0)
        _row_copy(src, xb_ref.at[pos1_ref[0, 0, r]], sem.at[slot]).start(priority=1)
        return c
    lax.fori_loop(0, td, issue, 0, unroll=8)

    @pl.when(i == n - 1)
    def _():
        wait_rows(slot)

        @pl.when(n >= 2)
        def _():
            wait_rows(1 - slot)


def _expert_kernel(first_ref, nblk_ref, tot_ref, xb_ref, w1_ref, w3_ref, w2_ref, yb_ref,
                   xbuf, ybuf, w1b, w3b, w2b, xsem, ysem):
    e = pl.program_id(0)
    nbu = tot_ref[0]
    half, blk_rows = xbuf.shape[1], xbuf.shape[2]

    def plane(ref, gb, s):
        return ref.at[pl.ds(gb * blk_rows, blk_rows), s, :]

    def xcopy(gb, slot, do):
        for s in range(half):
            do(pltpu.make_async_copy(plane(xb_ref, gb, s), xbuf.at[slot, s], xsem.at[slot]))

    def ycopy(gb, slot, do):
        for s in range(half):
            do(pltpu.make_async_copy(ybuf.at[slot, s], plane(yb_ref, gb, s), ysem.at[slot]))

    start = lambda c: c.start()
    wait = lambda c: c.wait()

    @pl.when(e == 0)
    def _():
        xcopy(0, 0, start)

    w1b[...] = w1_ref[...].astype(BF16)
    w3b[...] = w3_ref[...].astype(BF16)
    w2b[...] = w2_ref[...].astype(BF16)

    def body(b, carry):
        gb = first_ref[e] + b
        slot = gb % 2
        xcopy(gb, slot, wait)

        @pl.when(gb + 1 < nbu)
        def _():
            xcopy(gb + 1, 1 - slot, start)

        @pl.when(gb >= 2)
        def _():
            ycopy(gb - 2, slot, wait)

        x = _unpack_row_tiles([xbuf[slot, s] for s in range(half)]).astype(BF16)
        a = _dot(x, w1b[...])
        g = _dot(x, w3b[...])
        hid = (a * (1.0 / (1.0 + jnp.exp(-a)))) * g
        y = _dot(hid.astype(BF16), w2b[...])
        for s, words in enumerate(_pack_row_tiles(y)):
            ybuf[slot, s] = words
        ycopy(gb, slot, start)
        return carry
    lax.fori_loop(0, nblk_ref[e], body, 0)

    @pl.when(e == pl.num_programs(0) - 1)
    def _():
        ycopy(nbu - 2, nbu % 2, wait)
        ycopy(nbu - 1, (nbu - 1) % 2, wait)
        ybuf[0] = jnp.zeros(ybuf.shape[1:], ybuf.dtype)

        def fill(do):
            def tail(c, carry):
                ycopy(nbu + c, 0, do)
                return carry
            lax.fori_loop(0, tot_ref[1], tail, 0)
        fill(start)
        fill(wait)


def _combine_kernel(x_ref, g_ref, pos0_ref, pos1_ref, pos0n_ref, pos1n_ref, yb_ref, o_ref, ybuf, sem):
    i = pl.program_id(0)
    n = pl.num_programs(0)
    tf = x_ref.shape[0]
    slot = i % 2
    half = yb_ref.shape[1]

    def slab(s, k, r):
        return ybuf.at[s, k, pl.ds(pl.multiple_of(r * SUBLANES, SUBLANES), half)]

    def issue(s, p0_ref, p1_ref):
        def body(r, c):
            _row_copy(yb_ref.at[p0_ref[0, 0, r]], slab(s, 0, r), sem.at[s]).start(priority=0)
            _row_copy(yb_ref.at[p1_ref[0, 0, r]], slab(s, 1, r), sem.at[s]).start(priority=1)
            return c
        lax.fori_loop(0, tf, body, 0, unroll=8)

    @pl.when(i == 0)
    def _():
        issue(0, pos0_ref, pos1_ref)

    @pl.when(i + 1 < n)
    def _():
        issue(1 - slot, pos0n_ref, pos1n_ref)

    def wait_body(r, c):
        _row_copy(yb_ref.at[0], slab(slot, 0, 0), sem.at[slot]).wait()
        _row_copy(yb_ref.at[0], slab(slot, 1, 0), sem.at[slot]).wait()
        return c
    lax.fori_loop(0, tf, wait_body, 0, unroll=8)

    def rows_of(k):
        return _unpack_row_tiles([ybuf[slot, k, pl.ds(s, tf, stride=SUBLANES), :] for s in range(half)])

    g = g_ref[...]
    o_ref[...] = x_ref[...] + (g[:, 0:1] * rows_of(0) + g[:, 1:2] * rows_of(1))


def _const_spec(shape):
    return pl.BlockSpec(shape, lambda *_: (0,) * len(shape))


def _blockdiag_ones(width):
    r = jnp.arange(width) // HEAD_DIM
    return (r[:, None] == r[None, :]).astype(BF16)


def kernel(x, mem, norm_mix, w_in, q_norm, k_norm, attn_sinks, sgu_norm, sgu_w, sgu_b, out_norm_attn, out_norm_sgu, w_out, norm_cross, norm_mem, w_cq, w_ck, w_cv, cq_norm, ck_norm, w_co, norm_ffn, w_router_group, w_router_expert, w1, w3, w2):
    B, S, D = x.shape
    M = mem.shape[1]
    T = B * S
    A = T * TOP_K
    n_q = attn_sinks.shape[0]
    aw = n_q * HEAD_DIM
    sw = sgu_norm.shape[0]
    n_kv = (w_in.shape[1] - aw - 2 * sw) // (2 * HEAD_DIM)
    kvw = n_kv * HEAD_DIM
    chunk = sgu_w.shape[1]
    de = w1.shape[2]
    assert chunk == ATTN_BLOCK and S % SEQ_TILE == 0 and T % ROW_TILE == 0 and A % RANK_TILE == 0
    assert n_kv * 2 * HEAD_DIM == n_kv * LANES and sgu_w.shape[0] * HEAD_DIM == sw
    half = D // (2 * LANES)
    assert half * 2 * LANES == D
    row = lambda v: v.reshape(1, -1).astype(F32)
    cparams = functools.partial(pltpu.CompilerParams, vmem_limit_bytes=VMEM_LIMIT)

    dup = lambda w: jnp.repeat(w.reshape(D, n_kv, 1, HEAD_DIM), 2, axis=2).reshape(D, 2 * kvw)
    c0, c1, c2 = aw, aw + kvw, aw + 2 * kvw
    win = jnp.concatenate([w_in[:, :c0], dup(w_in[:, c0:c1]), dup(w_in[:, c1:c2]), w_in[:, c2:]], axis=1).astype(BF16)
    gq = row(jnp.tile(q_norm * k_norm * (HEAD_DIM ** -0.5), n_q))
    sguw_pairs = sgu_w.reshape(sw // LANES, 2 * chunk, chunk).astype(F32)
    sgub_full = jnp.repeat(sgu_b.T, HEAD_DIM, axis=1).astype(F32)
    hd = D // X_HEADS
    gqk = row(cq_norm * ck_norm * (hd ** -0.5))
    wr = jnp.concatenate([w_router_expert.transpose(0, 2, 1).reshape(N_EXPERTS, D), w_router_group.T,
                          jnp.zeros((EXPERTS_PER_GROUP - N_GROUPS, D), F32)], axis=0).astype(F32)
    wr_hi = wr.astype(BF16)
    wr_lo = (wr - wr_hi.astype(F32)).astype(BF16)
    wr1 = jnp.concatenate([wr_hi, wr_lo], axis=0)
    nr = wr.shape[0]

    kx, vx = pl.pallas_call(
        _memkv_kernel,
        grid=(B,),
        in_specs=[pl.BlockSpec((None, M, D), lambda b: (b, 0, 0)), _const_spec((1, D)), _const_spec((D, D)),
                  _const_spec((D, D))],
        out_specs=[pl.BlockSpec((None, M, D), lambda b: (b, 0, 0))] * 2,
        out_shape=[jax.ShapeDtypeStruct((B, M, D), BF16)] * 2,
        compiler_params=cparams(dimension_semantics=("arbitrary",)),
        name="memkv",
    )(mem, row(norm_mem), w_ck.astype(BF16), w_cv.astype(BF16))

    nseq = S // SEQ_TILE
    xtile = pl.BlockSpec((None, SEQ_TILE, D), lambda b, j: (b, j, 0))
    x1 = pl.pallas_call(
        _mixer_kernel,
        grid=(B, nseq),
        in_specs=[pl.BlockSpec(memory_space=pltpu.SMEM), xtile, _const_spec((1, D)), _const_spec(win.shape),
                  _const_spec((aw, aw)), _const_spec((2 * kvw, 2 * kvw)), _const_spec((1, aw)), _const_spec((1, sw)),
                  _const_spec(sguw_pairs.shape), _const_spec(sgub_full.shape), _const_spec((1, aw)),
                  _const_spec((1, sw)), _const_spec((D, D))],
        out_specs=xtile,
        out_shape=jax.ShapeDtypeStruct((B, S, D), F32),
        scratch_shapes=[pltpu.VMEM((ATTN_BLOCK, 2 * kvw), BF16), pltpu.VMEM((ATTN_BLOCK, 2 * kvw), BF16)],
        compiler_params=cparams(dimension_semantics=("arbitrary", "arbitrary")),
        name="mixer",
    )(attn_sinks.astype(F32), x, row(norm_mix), win, _blockdiag_ones(aw), _blockdiag_ones(2 * kvw), gq,
      row(sgu_norm), sguw_pairs, sgub_full, row(out_norm_attn), row(out_norm_sgu), w_out.astype(BF16))

    x2, ids, gcol, cnt = pl.pallas_call(
        _xattn_kernel,
        grid=(B, nseq),
        in_specs=[xtile, _const_spec((1, D)), _const_spec((D, D)), _const_spec((1, hd)),
                  pl.BlockSpec((None, M, D), lambda b, j: (b, 0, 0)), pl.BlockSpec((None, M, D), lambda b, j: (b, 0, 0)),
                  _const_spec((D, D)), _const_spec((1, D)), _const_spec((2 * nr, D)), _const_spec((nr, D))],
        out_specs=[xtile, pl.BlockSpec((EXPERTS_PER_GROUP, SEQ_TILE), lambda b, j: (0, b * nseq + j)),
                   pl.BlockSpec((SEQ_TILE, LANES), lambda b, j: (b * nseq + j, 0)), _const_spec((N_EXPERTS, LANES))],
        out_shape=[jax.ShapeDtypeStruct((B, S, D), F32), jax.ShapeDtypeStruct((EXPERTS_PER_GROUP, T), I32),
                   jax.ShapeDtypeStruct((T, LANES), F32), jax.ShapeDtypeStruct((N_EXPERTS, LANES), F32)],
        compiler_params=cparams(dimension_semantics=("arbitrary", "arbitrary")),
        name="xattn",
    )(x1, row(norm_cross), w_cq.astype(BF16), gqk, kx, vx, w_co.astype(BF16), row(norm_ffn), wr1, wr_hi)
    x2 = x2.reshape(T, D)
    ids = ids[:TOP_K]

    counts = cnt[:, 0].astype(I32)
    padded = (counts + MOE_ROWS - 1) // MOE_ROWS * MOE_ROWS
    pad_end = jnp.cumsum(padded)
    pad_start = pad_end - padded
    nb = A // MOE_ROWS + N_EXPERTS
    nbu = (pad_end[-1:] // MOE_ROWS).astype(I32)
    first_blk = (pad_start // MOE_ROWS).astype(I32)
    n_blk = (padded // MOE_ROWS).astype(I32)
    blk_tot = jnp.concatenate([nbu, nb - nbu]).astype(I32)
    fill_start = (pad_start + counts).astype(I32)
    fill_len = (padded - counts).astype(I32)
    zrows = MOE_ROWS // 2
    tail = jnp.concatenate([pad_end[-1:], (nb - nbu) * (MOE_ROWS // zrows)]).astype(I32)

    pos = pl.pallas_call(
        _rank_kernel,
        grid=(A // RANK_TILE,),
        in_specs=[pl.BlockSpec((1, RANK_TILE), lambda i: (0, i)), _const_spec((N_EXPERTS, LANES))],
        out_specs=pl.BlockSpec((1, RANK_TILE), lambda i: (0, i)),
        out_shape=jax.ShapeDtypeStruct((1, A), I32),
        scratch_shapes=[pltpu.VMEM((N_EXPERTS, LANES), F32), pltpu.VMEM((RANK_TILE, RANK_TILE), BF16)],
        compiler_params=cparams(dimension_semantics=("arbitrary",)),
        name="rank",
    )(ids.reshape(1, A), jnp.broadcast_to(pad_start.astype(F32)[:, None], (N_EXPERTS, LANES)))
    pos = pos.reshape(TOP_K, T // ROW_TILE, 1, ROW_TILE)

    nrow = T // ROW_TILE
    P = nb * MOE_ROWS
    pos_spec = pl.BlockSpec((1, 1, ROW_TILE), lambda i, *_: (i, 0, 0), memory_space=pltpu.SMEM)
    xb = pl.pallas_call(
        _dispatch_kernel,
        grid_spec=pltpu.PrefetchScalarGridSpec(
            num_scalar_prefetch=3,
            grid=(nrow,),
            in_specs=[pl.BlockSpec((ROW_TILE, D), lambda i, *_: (i, 0)), pl.BlockSpec((1, D), lambda i, *_: (0, 0)),
                      pos_spec, pos_spec],
            out_specs=pl.BlockSpec(memory_space=pl.ANY),
            scratch_shapes=[pltpu.VMEM((2, ROW_TILE * SUBLANES, LANES), U32), pltpu.VMEM((zrows, half, LANES), U32),
                            pltpu.SemaphoreType.DMA((2,)), pltpu.SemaphoreType.DMA(())],
        ),
        out_shape=jax.ShapeDtypeStruct((P, half, LANES), U32),
        compiler_params=cparams(dimension_semantics=("arbitrary",)),
        name="dispatch",
    )(fill_start, fill_len, tail, x2, row(norm_ffn), pos[0], pos[1])

    wmap = lambda e, *_: (e, 0, 0)
    blk_buf = pltpu.VMEM((2, half, MOE_ROWS, LANES), U32)
    yb = pl.pallas_call(
        _expert_kernel,
        grid_spec=pltpu.PrefetchScalarGridSpec(
            num_scalar_prefetch=3,
            grid=(N_EXPERTS,),
            in_specs=[pl.BlockSpec(memory_space=pl.ANY), pl.BlockSpec((None, D, de), wmap),
                      pl.BlockSpec((None, D, de), wmap), pl.BlockSpec((None, de, D), wmap)],
            out_specs=pl.BlockSpec(memory_space=pl.ANY),
            scratch_shapes=[blk_buf, blk_buf, pltpu.VMEM((D, de), BF16), pltpu.VMEM((D, de), BF16),
                            pltpu.VMEM((de, D), BF16), pltpu.SemaphoreType.DMA((2,)), pltpu.SemaphoreType.DMA((2,))],
        ),
        out_shape=jax.ShapeDtypeStruct((P, half, LANES), U32),
        compiler_params=cparams(dimension_semantics=("arbitrary",)),
        name="experts",
    )(first_blk, n_blk, blk_tot, xb, w1, w3, w2)

    pos_next = pl.BlockSpec((1, 1, ROW_TILE), lambda i: (jnp.minimum(i + 1, nrow - 1), 0, 0), memory_space=pltpu.SMEM)
    pos_cur = pl.BlockSpec((1, 1, ROW_TILE), lambda i: (i, 0, 0), memory_space=pltpu.SMEM)
    out = pl.pallas_call(
        _combine_kernel,
        grid=(nrow,),
        in_specs=[pl.BlockSpec((ROW_TILE, D), lambda i: (i, 0)), pl.BlockSpec((ROW_TILE, LANES), lambda i: (i, 0)),
                  pos_cur, pos_cur, pos_next, pos_next, pl.BlockSpec(memory_space=pl.ANY)],
        out_specs=pl.BlockSpec((ROW_TILE, D), lambda i: (i, 0)),
        out_shape=jax.ShapeDtypeStruct((T, D), F32),
        scratch_shapes=[pltpu.VMEM((2, TOP_K, ROW_TILE * SUBLANES, LANES), U32), pltpu.SemaphoreType.DMA((2,))],
        compiler_params=cparams(dimension_semantics=("arbitrary",)),
        name="combine",
    )(x2, gcol, pos[0], pos[1], pos[0], pos[1], yb)
    return out.reshape(B, S, D)
```

```python
import functools

import jax
import jax.numpy as jnp
from jax import lax
from jax.experimental import pallas as pl
from jax.experimental.pallas import tpu as pltpu

F32 = jnp.float32
BF16 = jnp.bfloat16
I32 = jnp.int32
U32 = jnp.uint32

EPS = 1e-6
HEAD_DIM = 64
ATTN_BLOCK = 128
X_HEADS = 4
N_GROUPS = 4
EXPERTS_PER_GROUP = 8
N_EXPERTS = N_GROUPS * EXPERTS_PER_GROUP
TOP_K = 2
LANES = 128
SUBLANES = 8

SEQ_TILE = 512
XATTN_TILE = 1024
MOE_ROWS = 512
EXPERT_RING = 3
RANK_TILE = 1024
ROW_TILE = 512
NEG = -1e30
LOG2E = 1.4426950408889634
VMEM_LIMIT = 56 * 1024 * 1024


def _dot(a, b):
    return lax.dot_general(a, b, (((1,), (0,)), ((), ())), preferred_element_type=F32)


def _dot_nt(a, b):
    return lax.dot_general(a, b, (((1,), (1,)), ((), ())), preferred_element_type=F32)


def _rms(x, gain):
    return x * lax.rsqrt(jnp.mean(x * x, axis=-1, keepdims=True) + EPS) * gain


def _rms_unit(x):
    return x * lax.rsqrt(jnp.mean(x * x, axis=-1, keepdims=True) + EPS)


def _gelu(x):
    c = 0.7978845608028654
    half_x = 0.5 * x
    return half_x + half_x * jnp.tanh(x * (c + (c * 0.044715) * (x * x)))


def _head_meansq(v, mean_blockdiag):
    w = mean_blockdiag.shape[0]
    sq = (v * v).astype(BF16)
    return jnp.concatenate([_dot(sq[:, c:c + w], mean_blockdiag) for c in range(0, v.shape[1], w)], axis=1)


def _pack_bf16_pair(lo, hi):
    def bits(v):
        u = lax.bitcast_convert_type(v, U32)
        return (u + U32(0x7FFF) + ((u >> 16) & U32(1))) >> 16
    return bits(lo) | (bits(hi) << 16)


def _unpack_bf16_pair(w):
    return lax.bitcast_convert_type(w << 16, F32), lax.bitcast_convert_type(w & U32(0xFFFF0000), F32)


def _pack_row_tiles(v):
    half = v.shape[1] // (2 * LANES)
    return [_pack_bf16_pair(v[:, s * LANES:(s + 1) * LANES], v[:, (s + half) * LANES:(s + half + 1) * LANES])
            for s in range(half)]


def _unpack_row_tiles(words):
    pairs = [_unpack_bf16_pair(w) for w in words]
    return jnp.concatenate([p[0] for p in pairs] + [p[1] for p in pairs], axis=1)


def _memkv_kernel(mem_ref, nmem_ref, wck_ref, wcv_ref, k_ref, v_ref):
    d = mem_ref.shape[-1]
    hd = d // X_HEADS
    mn = _rms(mem_ref[...], nmem_ref[...]).astype(BF16)
    k = _dot(mn, wck_ref[...])
    parts = []
    for h in range(X_HEADS):
        kh = k[:, h * hd:(h + 1) * hd]
        parts.append(kh * lax.rsqrt(jnp.mean(kh * kh, axis=-1, keepdims=True) + EPS))
    k_ref[...] = jnp.concatenate(parts, axis=1).astype(BF16)
    v_ref[...] = _dot(mn, wcv_ref[...]).astype(BF16)


def _mixer_kernel(sinks_ref, x_ref, win_ref, bdq_ref, bdk_ref, gq_ref, sgun_ref, sguw_ref, sgub_ref,
                  wout_ref, o_ref, kprev, vprev):
    j = pl.program_id(1)
    ts = x_ref.shape[0]
    nblk = ts // ATTN_BLOCK
    aw = gq_ref.shape[-1]
    kvw = bdk_ref.shape[-1]
    sw = sgun_ref.shape[-1]

    @pl.when(j == 0)
    def _():
        kprev[...] = jnp.zeros_like(kprev)
        vprev[...] = jnp.zeros_like(vprev)

    x = x_ref[...]
    h = _rms_unit(x).astype(BF16)
    proj = _dot(h, win_ref[...])
    c0, c1, c2, c3 = aw, aw + kvw, aw + 2 * kvw, aw + 2 * kvw + sw
    q = proj[:, :c0]
    k = proj[:, c0:c1]
    v = proj[:, c1:c2]
    u = _gelu(proj[:, c2:c3])
    g = _gelu(proj[:, c3:])

    qn = q * lax.rsqrt(_head_meansq(q, bdq_ref[...]) + EPS) * gq_ref[...]
    kn = k * lax.rsqrt(_head_meansq(k, bdk_ref[...]) + EPS)
    lane = lax.broadcasted_iota(I32, (1, LANES), 1)
    even_l = lane < HEAD_DIM

    def dup_heads(a):
        swapped = pltpu.roll(a, HEAD_DIM, 1)
        return jnp.concatenate([jnp.where(even_l, a, swapped), jnp.where(even_l, swapped, a)], axis=1).astype(BF16)
    kd_b = dup_heads(kn)
    vd_b = dup_heads(v)
    lane_q = lax.broadcasted_iota(I32, (1, aw), 1)
    even_q = (lane_q & HEAD_DIM) == 0
    q_even = jnp.where(even_q, qn, 0.0).astype(BF16)
    q_odd = jnp.where(even_q, 0.0, qn).astype(BF16)

    qi = lax.broadcasted_iota(I32, (ATTN_BLOCK, 2 * ATTN_BLOCK), 0)
    kj = lax.broadcasted_iota(I32, (ATTN_BLOCK, 2 * ATTN_BLOCK), 1)
    band = (kj > qi) & (kj <= qi + ATTN_BLOCK)
    first_key = jnp.where(j > 0, 0, ATTN_BLOCK)
    band_first = band & (kj >= first_key)

    n_kv = kvw // HEAD_DIM
    rep = aw // kvw
    attn_rows = []
    for n in range(nblk):
        rows = slice(n * ATTN_BLOCK, (n + 1) * ATTN_BLOCK)
        prev = slice((n - 1) * ATTN_BLOCK, n * ATTN_BLOCK)
        mask = band_first if n == 0 else band
        cols = []
        for hk in range(n_kv):
            kvc = slice(hk * LANES, (hk + 1) * LANES)
            k_prev = kprev[:, kvc] if n == 0 else kd_b[prev, kvc]
            v_prev = vprev[:, kvc] if n == 0 else vd_b[prev, kvc]
            kwin = jnp.concatenate([k_prev, kd_b[rows, kvc]], axis=0)
            vwin = jnp.concatenate([v_prev, vd_b[rows, kvc]], axis=0)
            qparts = []
            for r in range(rep):
                hq = hk * rep + r
                pc = slice((hq // 2) * LANES, (hq // 2 + 1) * LANES)
                qparts.append((q_even if hq % 2 == 0 else q_odd)[rows, pc])
            s = _dot_nt(jnp.concatenate(qparts, axis=0), kwin)
            ps, dens = [], []
            for r in range(rep):
                sr = jnp.where(mask, s[r * ATTN_BLOCK:(r + 1) * ATTN_BLOCK], NEG)
                sink = sinks_ref[hk * rep + r] * LOG2E
                m = jnp.maximum(jnp.max(sr, axis=-1, keepdims=True), sink)
                p = jnp.exp2(sr - m)
                dens.append(jnp.sum(p, axis=-1, keepdims=True) + jnp.exp2(sink - m))
                ps.append(p.astype(BF16))
            o = _dot(jnp.concatenate(ps, axis=0), vwin)
            outs = [o[r * ATTN_BLOCK:(r + 1) * ATTN_BLOCK] / dens[r] for r in range(rep)]
            for r in range(0, rep, 2):
                cols.append(jnp.where(even_l, outs[r], outs[r + 1]))
        attn_rows.append(jnp.concatenate(cols, axis=1))
    attn = jnp.concatenate(attn_rows, axis=0)
    last = slice(ts - ATTN_BLOCK, ts)
    kprev[...] = kd_b[last]
    vprev[...] = vd_b[last]

    vc_b = _rms(g, sgun_ref[...]).astype(BF16)
    ri = lax.broadcasted_iota(I32, (2 * ATTN_BLOCK, ATTN_BLOCK), 0)
    ci = lax.broadcasted_iota(I32, (2 * ATTN_BLOCK, ATTN_BLOCK), 1)
    causal = (ri & (ATTN_BLOCK - 1)) >= ci
    n_pair = sw // LANES
    wst = [jnp.where(causal, sguw_ref[p], 0.0).astype(BF16) for p in range(n_pair)]
    mixed = []
    for p in range(n_pair):
        pc = slice(p * LANES, (p + 1) * LANES)
        chunks = jnp.concatenate([vc_b[c * ATTN_BLOCK:(c + 1) * ATTN_BLOCK, pc] for c in range(nblk)], axis=1)
        res = _dot(wst[p], chunks)
        mixed.append(jnp.where(jnp.tile(even_l, (1, nblk)), res[:ATTN_BLOCK], res[ATTN_BLOCK:]))
    s_rows = [jnp.concatenate([m[:, c * LANES:(c + 1) * LANES] for m in mixed], axis=1) + sgub_ref[...]
              for c in range(nblk)]
    sg = u * jnp.concatenate(s_rows, axis=0)

    an = _rms_unit(attn).astype(BF16)
    sn = _rms_unit(sg).astype(BF16)
    o_ref[...] = x + _dot(an, wout_ref[:aw, :]) + _dot(sn, wout_ref[aw:, :])


def _xattn_kernel(x_ref, wcq_ref, gqk_ref, kx_ref, vx_ref, wco_ref, wr1_ref, wr2_ref,
                  o_ref, ids_ref, gcol_ref, cnt_ref):
    ts, d = x_ref.shape
    hd = d // X_HEADS
    x = x_ref[...]
    h = _rms_unit(x).astype(BF16)
    q = _dot(h, wcq_ref[...])
    outs = []
    for hh in range(X_HEADS):
        hc = slice(hh * hd, (hh + 1) * hd)
        qh = q[:, hc]
        qn = (qh * lax.rsqrt(jnp.mean(qh * qh, axis=-1, keepdims=True) + EPS) * gqk_ref[...]).astype(BF16)
        s = _dot_nt(qn, kx_ref[:, hc])
        p = jnp.exp2(s - jnp.max(s, axis=-1, keepdims=True))
        den = jnp.sum(p, axis=-1, keepdims=True)
        outs.append(_dot(p.astype(BF16), vx_ref[:, hc]) / den)
    x2 = x + _dot(jnp.concatenate(outs, axis=1).astype(BF16), wco_ref[...])
    o_ref[...] = x2

    h3 = _rms_unit(x2)
    h_hi = h3.astype(BF16)
    h_lo = (h3 - h_hi.astype(F32)).astype(BF16)
    nr = wr2_ref.shape[0]
    r1 = _dot_nt(wr1_ref[...], h_hi)
    logits = r1[:nr] + r1[nr:] + _dot_nt(wr2_ref[...], h_lo)
    ge = EXPERTS_PER_GROUP
    sub = lax.broadcasted_iota(I32, (ge, ts), 0).astype(F32)
    grp = jnp.where(sub < N_GROUPS, logits[N_EXPERTS:N_EXPERTS + ge], NEG)
    gmax = jnp.max(grp, axis=0, keepdims=True)
    gidx = jnp.min(jnp.where(grp == gmax, sub, float(ge)), axis=0, keepdims=True)
    g_w = 1.0 / jnp.sum(jnp.exp(grp - gmax), axis=0, keepdims=True)
    local = logits[(N_GROUPS - 1) * ge:N_GROUPS * ge]
    for gi in range(N_GROUPS - 2, -1, -1):
        local = jnp.where(gidx == float(gi), logits[gi * ge:(gi + 1) * ge], local)
    m1 = jnp.max(local, axis=0, keepdims=True)
    i1 = jnp.min(jnp.where(local == m1, sub, float(ge)), axis=0, keepdims=True)
    rest = jnp.where(sub == i1, NEG, local)
    m2 = jnp.max(rest, axis=0, keepdims=True)
    i2 = jnp.min(jnp.where(rest == m2, sub, float(ge)), axis=0, keepdims=True)
    e2 = jnp.exp(m2 - m1)
    den = 1.0 + e2
    w1 = g_w * (1.0 / den)
    w2 = g_w * (e2 / den)
    id1, id2 = gidx * ge + i1, gidx * ge + i2
    ids_ref[...] = jnp.where(sub == 0.0, id1, jnp.where(sub == 1.0, id2, 0.0)).astype(I32)
    rows = lax.broadcasted_iota(I32, (LANES, ts), 0)
    gates_t = jnp.where(rows == 0, w1, jnp.where(rows == 1, w2, 0.0))
    gcol_ref[...] = gates_t.T

    ex = lax.broadcasted_iota(I32, (N_EXPERTS, ts), 0).astype(F32)
    hits = jnp.where(ex == id1, 1.0, 0.0) + jnp.where(ex == id2, 1.0, 0.0)

    @pl.when(jnp.logical_and(pl.program_id(0) == 0, pl.program_id(1) == 0))
    def _():
        cnt_ref[...] = jnp.zeros_like(cnt_ref)
    cnt_ref[...] = cnt_ref[...] + jnp.sum(hits, axis=1, keepdims=True)


def _rank_kernel(ids_ref, pstart_ref, pos_ref, carry, upper):
    i = pl.program_id(0)
    ct = ids_ref.shape[-1]

    @pl.when(i == 0)
    def _():
        carry[...] = jnp.zeros_like(carry)
        r = lax.broadcasted_iota(I32, (ct, ct), 0)
        c = lax.broadcasted_iota(I32, (ct, ct), 1)
        upper[...] = jnp.where(r < c, 1.0, 0.0).astype(BF16)

    e = ids_ref[...]
    onehot = lax.broadcasted_iota(I32, (N_EXPERTS, ct), 0) == e
    oh = jnp.where(onehot, 1.0, 0.0)
    row = _dot(oh.astype(BF16), upper[...]) + (carry[:, 0:1] + pstart_ref[:, 0:1])
    pos_ref[...] = jnp.sum(jnp.where(onehot, row, 0.0), axis=0, keepdims=True).astype(I32)
    carry[...] = carry[...] + jnp.sum(oh, axis=1, keepdims=True)


def _row_copy(src, dst, sem):
    return pltpu.make_async_copy(src, dst, sem)


def _dispatch_kernel(fs_ref, fl_ref, tail_ref, x_ref, nffn_ref, pos0_ref, pos1_ref, xb_ref, buf, zbuf, sem, zsem):
    i = pl.program_id(0)
    n = pl.num_programs(0)
    td = x_ref.shape[0]
    slot = i % 2
    zrows, half = zbuf.shape[0], zbuf.shape[1]
    chunk_bits = [1 << b for b in range((MOE_ROWS - 1).bit_length())]

    def slab(s, r):
        return buf.at[s, pl.ds(pl.multiple_of(r * SUBLANES, SUBLANES), half)]

    def wait_rows(s):
        def body(r, c):
            _row_copy(slab(s, 0), xb_ref.at[0], sem.at[s]).wait()
            _row_copy(slab(s, 0), xb_ref.at[0], sem.at[s]).wait()
            return c
        lax.fori_loop(0, td, body, 0, unroll=8)

    def zero_fill(e, do):
        start, length = fs_ref[e], fl_ref[e]
        for bit in chunk_bits:
            @pl.when((length & bit) != 0)
            def _(bit=bit):
                off = start + (length & ~(2 * bit - 1))
                do(_row_copy(zbuf.at[pl.ds(0, bit)], xb_ref.at[pl.ds(off, bit)], zsem))

    def zero_tail(do):
        def body(c, carry):
            do(_row_copy(zbuf, xb_ref.at[pl.ds(tail_ref[0] + c * zrows, zrows)], zsem))
            return carry
        lax.fori_loop(0, tail_ref[1], body, 0)

    def for_experts(do):
        def body(e, carry):
            zero_fill(e, do)
            return carry
        lax.fori_loop(0, N_EXPERTS, body, 0)

    @pl.when(i == 0)
    def _():
        zbuf[...] = jnp.zeros_like(zbuf)
        for_experts(lambda c: c.start())
        zero_tail(lambda c: c.start())
        for_experts(lambda c: c.wait())
        zero_tail(lambda c: c.wait())

    @pl.when(i >= 2)
    def _():
        wait_rows(slot)

    for s, words in enumerate(_pack_row_tiles(_rms(x_ref[...], nffn_ref[...]))):
        buf[slot, pl.ds(s, td, stride=SUBLANES), :] = words

    def issue(r, c):
        src = slab(slot, r)
        _row_copy(src, xb_ref.at[pos0_ref[0, 0, r]], sem.at[slot]).start(priority=0)
        _row_copy(src, xb_ref.at[pos1_ref[0, 0, r]], sem.at[slot]).start(priority=1)
        return c
    lax.fori_loop(0, td, issue, 0, unroll=8)

    @pl.when(i == n - 1)
    def _():
        wait_rows(slot)

        @pl.when(n >= 2)
        def _():
            wait_rows(1 - slot)


def _expert_kernel(first_ref, nblk_ref, tot_ref, xb_ref, w1_ref, w3_ref, w2_ref, yb_ref,
                   xbuf, ybuf, w1b, w3b, w2b, xsem, ysem):
    e = pl.program_id(0)
    nbu = tot_ref[0]
    nbuf, half, blk_rows = xbuf.shape[0], xbuf.shape[1], xbuf.shape[2]

    def plane(ref, gb, s):
        return ref.at[pl.ds(gb * blk_rows, blk_rows), s, :]

    def xcopy(gb, slot, do):
        for s in range(half):
            do(pltpu.make_async_copy(plane(xb_ref, gb, s), xbuf.at[slot, s], xsem.at[slot]))

    def ycopy(gb, slot, do):
        for s in range(half):
            do(pltpu.make_async_copy(ybuf.at[slot, s], plane(yb_ref, gb, s), ysem.at[slot]))

    start = lambda c: c.start()
    wait = lambda c: c.wait()

    @pl.when(e == 0)
    def _():
        for gb in range(nbuf - 1):
            xcopy(gb, gb, start)

    w1b[...] = w1_ref[...].astype(BF16)
    w3b[...] = w3_ref[...].astype(BF16)
    w2b[...] = w2_ref[...].astype(BF16)

    def body(b, carry):
        gb = first_ref[e] + b
        slot = lax.rem(gb, nbuf)
        xcopy(gb, slot, wait)

        @pl.when(gb + nbuf - 1 < nbu)
        def _():
            xcopy(gb + nbuf - 1, lax.rem(gb + nbuf - 1, nbuf), start)

        @pl.when(gb >= nbuf)
        def _():
            ycopy(gb - nbuf, slot, wait)

        x = _unpack_row_tiles([xbuf[slot, s] for s in range(half)]).astype(BF16)
        a = _dot(x, w1b[...])
        g = _dot(x, w3b[...])
        hid = (a * (1.0 / (1.0 + jnp.exp(-a)))) * g
        y = _dot(hid.astype(BF16), w2b[...])
        for s, words in enumerate(_pack_row_tiles(y)):
            ybuf[slot, s] = words
        ycopy(gb, slot, start)
        return carry
    lax.fori_loop(0, nblk_ref[e], body, 0)

    @pl.when(e == pl.num_programs(0) - 1)
    def _():
        for back in range(nbuf, 0, -1):
            ycopy(nbu - back, lax.rem(nbu - back, nbuf), wait)
        ybuf[0] = jnp.zeros(ybuf.shape[1:], ybuf.dtype)

        def fill(do):
            def tail(c, carry):
                ycopy(nbu + c, 0, do)
                return carry
            lax.fori_loop(0, tot_ref[1], tail, 0)
        fill(start)
        fill(wait)


def _combine_kernel(x_ref, g_ref, pos0_ref, pos1_ref, pos0n_ref, pos1n_ref, yb_ref, o_ref, ybuf, sem):
    i = pl.program_id(0)
    n = pl.num_programs(0)
    tf = x_ref.shape[0]
    slot = i % 2
    half = yb_ref.shape[1]

    def slab(s, k, r):
        return ybuf.at[s, k, pl.ds(pl.multiple_of(r * SUBLANES, SUBLANES), half)]

    def issue(s, p0_ref, p1_ref):
        def body(r, c):
            _row_copy(yb_ref.at[p0_ref[0, 0, r]], slab(s, 0, r), sem.at[s]).start(priority=0)
            _row_copy(yb_ref.at[p1_ref[0, 0, r]], slab(s, 1, r), sem.at[s]).start(priority=1)
            return c
        lax.fori_loop(0, tf, body, 0, unroll=8)

    @pl.when(i == 0)
    def _():
        issue(0, pos0_ref, pos1_ref)

    @pl.when(i + 1 < n)
    def _():
        issue(1 - slot, pos0n_ref, pos1n_ref)

    def wait_body(r, c):
        _row_copy(yb_ref.at[0], slab(slot, 0, 0), sem.at[slot]).wait()
        _row_copy(yb_ref.at[0], slab(slot, 1, 0), sem.at[slot]).wait()
        return c
    lax.fori_loop(0, tf, wait_body, 0, unroll=8)

    def rows_of(k):
        return _unpack_row_tiles([ybuf[slot, k, pl.ds(s, tf, stride=SUBLANES), :] for s in range(half)])

    g = g_ref[...]
    o_ref[...] = x_ref[...] + (g[:, 0:1] * rows_of(0) + g[:, 1:2] * rows_of(1))


def _const_spec(shape):
    return pl.BlockSpec(shape, lambda *_: (0,) * len(shape))


def _blockdiag_mean(width):
    r = jnp.arange(width) // HEAD_DIM
    return jnp.where(r[:, None] == r[None, :], 1.0 / HEAD_DIM, 0.0).astype(BF16)


def kernel(x, mem, norm_mix, w_in, q_norm, k_norm, attn_sinks, sgu_norm, sgu_w, sgu_b, out_norm_attn, out_norm_sgu, w_out, norm_cross, norm_mem, w_cq, w_ck, w_cv, cq_norm, ck_norm, w_co, norm_ffn, w_router_group, w_router_expert, w1, w3, w2):
    B, S, D = x.shape
    M = mem.shape[1]
    T = B * S
    A = T * TOP_K
    n_q = attn_sinks.shape[0]
    aw = n_q * HEAD_DIM
    sw = sgu_norm.shape[0]
    n_kv = (w_in.shape[1] - aw - 2 * sw) // (2 * HEAD_DIM)
    kvw = n_kv * HEAD_DIM
    chunk = sgu_w.shape[1]
    de = w1.shape[2]
    assert chunk == ATTN_BLOCK and S % SEQ_TILE == 0 and S % XATTN_TILE == 0 and T % ROW_TILE == 0 and A % RANK_TILE == 0
    assert n_kv * 2 * HEAD_DIM == n_kv * LANES and sgu_w.shape[0] * HEAD_DIM == sw
    half = D // (2 * LANES)
    assert half * 2 * LANES == D
    row = lambda v: v.reshape(1, -1).astype(F32)
    cparams = functools.partial(pltpu.CompilerParams, vmem_limit_bytes=VMEM_LIMIT)

    win = (norm_mix[:, None] * w_in).astype(BF16)
    gq = row(jnp.tile(q_norm * k_norm * (HEAD_DIM ** -0.5 * LOG2E), n_q))
    wout = (jnp.concatenate([out_norm_attn, out_norm_sgu])[:, None] * w_out).astype(BF16)
    sguw_pairs = sgu_w.reshape(sw // LANES, 2 * chunk, chunk).astype(F32)
    sgub_full = jnp.repeat(sgu_b.T, HEAD_DIM, axis=1).astype(F32)
    hd = D // X_HEADS
    gqk = row(cq_norm * ck_norm * (hd ** -0.5 * LOG2E))
    wcq = (norm_cross[:, None] * w_cq).astype(BF16)
    wr = jnp.concatenate([w_router_expert.transpose(0, 2, 1).reshape(N_EXPERTS, D), w_router_group.T,
                          jnp.zeros((EXPERTS_PER_GROUP - N_GROUPS, D), F32)], axis=0).astype(F32)
    wr = wr * norm_ffn[None, :]
    wr_hi = wr.astype(BF16)
    wr_lo = (wr - wr_hi.astype(F32)).astype(BF16)
    wr1 = jnp.concatenate([wr_hi, wr_lo], axis=0)
    nr = wr.shape[0]

    kx, vx = pl.pallas_call(
        _memkv_kernel,
        grid=(B,),
        in_specs=[pl.BlockSpec((None, M, D), lambda b: (b, 0, 0)), _const_spec((1, D)), _const_spec((D, D)),
                  _const_spec((D, D))],
        out_specs=[pl.BlockSpec((None, M, D), lambda b: (b, 0, 0))] * 2,
        out_shape=[jax.ShapeDtypeStruct((B, M, D), BF16)] * 2,
        compiler_params=cparams(dimension_semantics=("arbitrary",)),
        name="memkv",
    )(mem, row(norm_mem), w_ck.astype(BF16), w_cv.astype(BF16))

    nseq = S // SEQ_TILE
    xtile = pl.BlockSpec((None, SEQ_TILE, D), lambda b, j: (b, j, 0))
    x1 = pl.pallas_call(
        _mixer_kernel,
        grid=(B, nseq),
        in_specs=[pl.BlockSpec(memory_space=pltpu.SMEM), xtile, _const_spec(win.shape),
                  _const_spec((2 * LANES, 2 * LANES)), _const_spec((kvw, kvw)), _const_spec((1, aw)), _const_spec((1, sw)),
                  _const_spec(sguw_pairs.shape), _const_spec(sgub_full.shape), _const_spec((D, D))],
        out_specs=xtile,
        out_shape=jax.ShapeDtypeStruct((B, S, D), F32),
        scratch_shapes=[pltpu.VMEM((ATTN_BLOCK, 2 * kvw), BF16), pltpu.VMEM((ATTN_BLOCK, 2 * kvw), BF16)],
        compiler_params=cparams(dimension_semantics=("arbitrary", "arbitrary")),
        name="mixer",
    )(attn_sinks.astype(F32), x, win, _blockdiag_mean(2 * LANES), _blockdiag_mean(kvw), gq,
      row(sgu_norm), sguw_pairs, sgub_full, wout)

    nxt = S // XATTN_TILE
    xtile2 = pl.BlockSpec((None, XATTN_TILE, D), lambda b, j: (b, j, 0))
    x2, ids, gcol, cnt = pl.pallas_call(
        _xattn_kernel,
        grid=(B, nxt),
        in_specs=[xtile2, _const_spec((D, D)), _const_spec((1, hd)),
                  pl.BlockSpec((None, M, D), lambda b, j: (b, 0, 0)), pl.BlockSpec((None, M, D), lambda b, j: (b, 0, 0)),
                  _const_spec((D, D)), _const_spec((2 * nr, D)), _const_spec((nr, D))],
        out_specs=[xtile2, pl.BlockSpec((EXPERTS_PER_GROUP, XATTN_TILE), lambda b, j: (0, b * nxt + j)),
                   pl.BlockSpec((XATTN_TILE, LANES), lambda b, j: (b * nxt + j, 0)), _const_spec((N_EXPERTS, LANES))],
        out_shape=[jax.ShapeDtypeStruct((B, S, D), F32), jax.ShapeDtypeStruct((EXPERTS_PER_GROUP, T), I32),
                   jax.ShapeDtypeStruct((T, LANES), F32), jax.ShapeDtypeStruct((N_EXPERTS, LANES), F32)],
        compiler_params=cparams(dimension_semantics=("arbitrary", "arbitrary")),
        name="xattn",
    )(x1, wcq, gqk, kx, vx, w_co.astype(BF16), wr1, wr_hi)
    x2 = x2.reshape(T, D)
    ids = ids[:TOP_K]

    counts = cnt[:, 0].astype(I32)
    padded = (counts + MOE_ROWS - 1) // MOE_ROWS * MOE_ROWS
    pad_end = jnp.cumsum(padded)
    pad_start = pad_end - padded
    nb = A // MOE_ROWS + N_EXPERTS
    nbu = (pad_end[-1:] // MOE_ROWS).astype(I32)
    first_blk = (pad_start // MOE_ROWS).astype(I32)
    n_blk = (padded // MOE_ROWS).astype(I32)
    blk_tot = jnp.concatenate([nbu, nb - nbu]).astype(I32)
    fill_start = (pad_start + counts).astype(I32)
    fill_len = (padded - counts).astype(I32)
    zrows = MOE_ROWS // 2
    tail = jnp.concatenate([pad_end[-1:], (nb - nbu) * (MOE_ROWS // zrows)]).astype(I32)

    pos = pl.pallas_call(
        _rank_kernel,
        grid=(A // RANK_TILE,),
        in_specs=[pl.BlockSpec((1, RANK_TILE), lambda i: (0, i)), _const_spec((N_EXPERTS, LANES))],
        out_specs=pl.BlockSpec((1, RANK_TILE), lambda i: (0, i)),
        out_shape=jax.ShapeDtypeStruct((1, A), I32),
        scratch_shapes=[pltpu.VMEM((N_EXPERTS, LANES), F32), pltpu.VMEM((RANK_TILE, RANK_TILE), BF16)],
        compiler_params=cparams(dimension_semantics=("arbitrary",)),
        name="rank",
    )(ids.reshape(1, A), jnp.broadcast_to(pad_start.astype(F32)[:, None], (N_EXPERTS, LANES)))
    pos = pos.reshape(TOP_K, T // ROW_TILE, 1, ROW_TILE)

    nrow = T // ROW_TILE
    P = nb * MOE_ROWS
    pos_spec = pl.BlockSpec((1, 1, ROW_TILE), lambda i, *_: (i, 0, 0), memory_space=pltpu.SMEM)
    xb = pl.pallas_call(
        _dispatch_kernel,
        grid_spec=pltpu.PrefetchScalarGridSpec(
            num_scalar_prefetch=3,
            grid=(nrow,),
            in_specs=[pl.BlockSpec((ROW_TILE, D), lambda i, *_: (i, 0)), pl.BlockSpec((1, D), lambda i, *_: (0, 0)),
                      pos_spec, pos_spec],
            out_specs=pl.BlockSpec(memory_space=pl.ANY),
            scratch_shapes=[pltpu.VMEM((2, ROW_TILE * SUBLANES, LANES), U32), pltpu.VMEM((zrows, half, LANES), U32),
                            pltpu.SemaphoreType.DMA((2,)), pltpu.SemaphoreType.DMA(())],
        ),
        out_shape=jax.ShapeDtypeStruct((P, half, LANES), U32),
        compiler_params=cparams(dimension_semantics=("arbitrary",)),
        name="dispatch",
    )(fill_start, fill_len, tail, x2, row(norm_ffn), pos[0], pos[1])

    wmap = lambda e, *_: (e, 0, 0)
    blk_buf = pltpu.VMEM((EXPERT_RING, half, MOE_ROWS, LANES), U32)
    ring_sem = pltpu.SemaphoreType.DMA((EXPERT_RING,))
    yb = pl.pallas_call(
        _expert_kernel,
        grid_spec=pltpu.PrefetchScalarGridSpec(
            num_scalar_prefetch=3,
            grid=(N_EXPERTS,),
            in_specs=[pl.BlockSpec(memory_space=pl.ANY), pl.BlockSpec((None, D, de), wmap),
                      pl.BlockSpec((None, D, de), wmap), pl.BlockSpec((None, de, D), wmap)],
            out_specs=pl.BlockSpec(memory_space=pl.ANY),
            scratch_shapes=[blk_buf, blk_buf, pltpu.VMEM((D, de), BF16), pltpu.VMEM((D, de), BF16),
                            pltpu.VMEM((de, D), BF16), ring_sem, ring_sem],
        ),
        out_shape=jax.ShapeDtypeStruct((P, half, LANES), U32),
        compiler_params=cparams(dimension_semantics=("arbitrary",)),
        name="experts",
    )(first_blk, n_blk, blk_tot, xb, w1, w3, w2)

    pos_next = pl.BlockSpec((1, 1, ROW_TILE), lambda i: (jnp.minimum(i + 1, nrow - 1), 0, 0), memory_space=pltpu.SMEM)
    pos_cur = pl.BlockSpec((1, 1, ROW_TILE), lambda i: (i, 0, 0), memory_space=pltpu.SMEM)
    out = pl.pallas_call(
        _combine_kernel,
        grid=(nrow,),
        in_specs=[pl.BlockSpec((ROW_TILE, D), lambda i: (i, 0)), pl.BlockSpec((ROW_TILE, LANES), lambda i: (i, 0)),
                  pos_cur, pos_cur, pos_next, pos_next, pl.BlockSpec(memory_space=pl.ANY)],
        out_specs=pl.BlockSpec((ROW_TILE, D), lambda i: (i, 0)),
        out_shape=jax.ShapeDtypeStruct((T, D), F32),
        scratch_shapes=[pltpu.VMEM((2, TOP_K, ROW_TILE * SUBLANES, LANES), U32), pltpu.SemaphoreType.DMA((2,))],
        compiler_params=cparams(dimension_semantics=("arbitrary",)),
        name="combine",
    )(x2, gcol, pos[0], pos[1], pos[0], pos[1], yb)
    return out.reshape(B, S, D)
```
